```python
import jax, jax.numpy as jnp
from jax import lax
import numpy as np

D_MODEL = 1024
BATCH = 4
SEQ = 8192
DEPTH = 1

M_HEADS = 4
M_HEAD_DIM = 128
M_WIDTH = M_HEADS * M_HEAD_DIM
CONV_W = 4
CHUNK = 128
A_HEADS = 8
A_HEAD_DIM = 64
A_WIDTH = A_HEADS * A_HEAD_DIM
DIL_PATTERNS = ((128, 1), (512, 4), (2048, 16))
N_BUCKETS = 32
MAX_DISTANCE = 2048
MIX_WIDTH = M_WIDTH + A_WIDTH
IN_SIZES = (M_WIDTH, M_WIDTH, M_WIDTH, M_HEADS, M_HEADS, A_WIDTH, A_WIDTH, A_WIDTH)
IN_COLS = 3 * M_WIDTH + 2 * M_HEADS + 3 * A_WIDTH
PEER_HEADS = 8
N_KEYS = 128
N_EXPERTS = N_KEYS * N_KEYS
PEER_QDIM = 256
PEER_TOPK = 16
PEER_BLOCK = 128
EPS = 1e-6

kernel_name = 'hymba_mlstm_dilattn_peer_layer'


def rms_norm(x, g):
    xf = x.astype(jnp.float32)
    y = xf * lax.rsqrt(jnp.mean(xf * xf, axis=-1, keepdims=True) + EPS)
    return (y * g).astype(x.dtype)


def t5_bucket(dist):
    max_exact = N_BUCKETS // 2
    nf = jnp.maximum(dist, 1).astype(jnp.float32)
    large = max_exact + (jnp.log(nf / max_exact) / np.log(MAX_DISTANCE / max_exact)
                         * (N_BUCKETS - max_exact)).astype(jnp.int32)
    large = jnp.minimum(large, N_BUCKETS - 1)
    return jnp.where(dist < max_exact, dist, large)


def causal_short_conv(u, w, b):
    y = lax.conv_general_dilated(u, w[:, None, :], window_strides=(1,),
                                 padding=[(CONV_W - 1, 0)],
                                 dimension_numbers=('NWC', 'WIO', 'NWC'),
                                 feature_group_count=u.shape[-1])
    return y + b


def mlstm_chunkwise(q, k, v, ipre, fpre):
    B, H, S, Dh = q.shape
    nc = S // CHUNK
    q = q.reshape(B, H, nc, CHUNK, Dh)
    k = k.reshape(B, H, nc, CHUNK, Dh)
    v = v.reshape(B, H, nc, CHUNK, Dh)
    logi = ipre.reshape(B, H, nc, CHUNK)
    b = jnp.cumsum(jax.nn.log_sigmoid(fpre).reshape(B, H, nc, CHUNK), axis=-1)
    b_last = b[..., -1]
    g = b_last[..., None] - b + logi
    m_loc = jnp.max(g, axis=-1)
    wts = jnp.exp(g - m_loc[..., None])
    c_chunk = jnp.einsum('bhnl,bhnlk,bhnlv->bhnkv', wts, k, v)
    n_chunk = jnp.einsum('bhnl,bhnlk->bhnk', wts, k)

    def step(carry, inp):
        C, n, m = carry
        bl, ml, cc, nn = inp
        m_new = jnp.maximum(bl + m, ml)
        a = jnp.exp(bl + m - m_new)
        bb = jnp.exp(ml - m_new)
        C_new = a[..., None, None] * C + bb[..., None, None] * cc
        n_new = a[..., None] * n + bb[..., None] * nn
        return (C_new, n_new, m_new), (C, n, m)

    init = (jnp.zeros((B, H, Dh, Dh), jnp.float32), jnp.zeros((B, H, Dh), jnp.float32),
            jnp.zeros((B, H), jnp.float32))
    xs = (jnp.moveaxis(b_last, 2, 0), jnp.moveaxis(m_loc, 2, 0),
          jnp.moveaxis(c_chunk, 2, 0), jnp.moveaxis(n_chunk, 2, 0))
    _, (C_prev, n_prev, m_prev) = lax.scan(step, init, xs)
    C_prev = jnp.moveaxis(C_prev, 0, 2)
    n_prev = jnp.moveaxis(n_prev, 0, 2)
    m_prev = jnp.moveaxis(m_prev, 0, 2)

    causal = jnp.tril(jnp.ones((CHUNK, CHUNK), bool))
    log_d = jnp.where(causal, b[..., :, None] - b[..., None, :] + logi[..., None, :], -jnp.inf)
    inter = b + m_prev[..., None]
    m_t = jnp.maximum(inter, jnp.max(log_d, axis=-1))
    d_mat = jnp.exp(log_d - m_t[..., None])
    inter_w = jnp.exp(inter - m_t)
    s = jnp.einsum('bhnqd,bhnkd->bhnqk', q, k) * d_mat
    num = (jnp.einsum('bhnqk,bhnkd->bhnqd', s, v)
           + inter_w[..., None] * jnp.einsum('bhnqd,bhnde->bhnqe', q, C_prev))
    den = jnp.sum(s, axis=-1) + inter_w * jnp.einsum('bhnqd,bhnd->bhnq', q, n_prev)
    h = num / jnp.maximum(jnp.abs(den), jnp.exp(-m_t))[..., None]
    return h.reshape(B, H, S, Dh)


def dilated_branch(q, k, v, rel_bias, window, dilation):
    B, S, H, Dh = q.shape
    blk = window // dilation
    span = dilation * blk
    s_pad = -(-S // span) * span
    nb = s_pad // span

    def to_blocks(t):
        t = jnp.pad(t, ((0, 0), (0, s_pad - S), (0, 0), (0, 0)))
        return t.reshape(B, nb, blk, dilation, H, Dh).transpose(0, 3, 4, 1, 2, 5)

    qb, kb, vb = to_blocks(q), to_blocks(k), to_blocks(v)
    shift = lambda t: jnp.pad(t, ((0, 0), (0, 0), (0, 0), (1, 0), (0, 0), (0, 0)))[:, :, :, :-1]
    kk = jnp.concatenate([shift(kb), kb], axis=4)
    vv = jnp.concatenate([shift(vb), vb], axis=4)
    i = jnp.arange(blk)[:, None]
    j = jnp.arange(2 * blk)[None, :]
    steps = i + blk - j
    band = (steps >= 0) & (steps <= blk)
    valid = band[None] & ((jnp.arange(nb)[:, None, None] > 0) | (j >= blk)[None])
    bias = rel_bias[t5_bucket(jnp.maximum(steps, 0) * dilation)].transpose(2, 0, 1)
    s = (jnp.einsum('brhnqc,brhnkc->brhnqk', qb, kk).astype(jnp.float32) * (A_HEAD_DIM ** -0.5)
         + bias[None, None, :, None])
    s = jnp.where(valid[None, None, None], s, -jnp.inf)
    m = jnp.max(s, axis=-1, keepdims=True)
    p = jnp.exp(s - m)
    den = jnp.sum(p, axis=-1)
    o = jnp.einsum('brhnqk,brhnkc->brhnqc', p, vv) / den[..., None]
    lse = m[..., 0] + jnp.log(den)
    o = o.transpose(0, 3, 4, 1, 2, 5).reshape(B, s_pad, H, Dh)[:, :S]
    lse = lse.transpose(0, 3, 4, 1, 2).reshape(B, s_pad, H)[:, :S]
    return o, lse


def peer(h, w_query, sub_keys1, sub_keys2, expert_u, expert_v):
    B, S, D = h.shape
    T = B * S
    hf = h.reshape(T, D)
    q = (hf @ w_query).reshape(T, PEER_HEADS, 2, PEER_QDIM // 2)
    s1 = jnp.einsum('thc,hkc->thk', q[:, :, 0], sub_keys1).astype(jnp.float32)
    s2 = jnp.einsum('thc,hkc->thk', q[:, :, 1], sub_keys2).astype(jnp.float32)
    v1, i1 = lax.top_k(s1, PEER_TOPK)
    v2, i2 = lax.top_k(s2, PEER_TOPK)
    cand = (v1[..., :, None] + v2[..., None, :]).reshape(T, PEER_HEADS, PEER_TOPK * PEER_TOPK)
    vc, ic = lax.top_k(cand, PEER_TOPK)
    e1 = jnp.take_along_axis(i1, ic // PEER_TOPK, axis=-1)
    e2 = jnp.take_along_axis(i2, ic % PEER_TOPK, axis=-1)
    idx = (e1 * N_KEYS + e2).reshape(T, PEER_HEADS * PEER_TOPK)
    gate = jax.nn.softmax(vc, axis=-1).reshape(T, PEER_HEADS * PEER_TOPK)

    def block(args):
        xb, ib, gb = args
        u = expert_u[ib]
        pre = jnp.einsum('td,tkd->tk', xb, u).astype(jnp.float32)
        a = jax.nn.gelu(pre, approximate=False) * gb
        return jnp.einsum('tk,tkd->td', a, expert_v[ib])

    nblk = T // PEER_BLOCK
    out = lax.map(block, (hf.reshape(nblk, PEER_BLOCK, D),
                          idx.reshape(nblk, PEER_BLOCK, -1),
                          gate.reshape(nblk, PEER_BLOCK, -1)))
    return out.reshape(B, S, D).astype(h.dtype)


def setup_inputs(seed: int = 0) -> dict:
    key = jax.random.key(seed)
    ks = jax.random.split(key, 24)
    n = lambda i, shape: jax.random.normal(ks[i], shape, jnp.float32)
    return {
        'x': n(0, (BATCH, SEQ, D_MODEL)),
        'norm1_g': 1.0 + 0.02 * n(1, (DEPTH, D_MODEL)),
        'w_in': n(2, (DEPTH, D_MODEL, IN_COLS)) * D_MODEL ** -0.5,
        'conv_w': n(3, (DEPTH, CONV_W, M_WIDTH)) * CONV_W ** -0.5,
        'conv_b': 0.02 * n(4, (DEPTH, M_WIDTH)),
        'wq_m': n(5, (DEPTH, M_HEADS, M_HEAD_DIM, M_HEAD_DIM)) * M_HEAD_DIM ** -0.5,
        'wk_m': n(6, (DEPTH, M_HEADS, M_HEAD_DIM, M_HEAD_DIM)) * M_HEAD_DIM ** -0.5,
        'ig_b': 0.1 * n(7, (DEPTH, M_HEADS)),
        'fg_b': jnp.linspace(3.0, 6.0, M_HEADS, dtype=jnp.float32)[None] + 0.1 * n(8, (DEPTH, M_HEADS)),
        'mh_norm_g': 1.0 + 0.02 * n(9, (DEPTH, M_HEADS, M_HEAD_DIM)),
        'skip_m': 1.0 + 0.02 * n(10, (DEPTH, M_HEADS, M_HEAD_DIM)),
        'qn_g': 1.0 + 0.02 * n(11, (DEPTH, A_HEADS, A_HEAD_DIM)),
        'kn_g': 1.0 + 0.02 * n(12, (DEPTH, A_HEADS, A_HEAD_DIM)),
        'rel_bias': 0.5 * n(13, (N_BUCKETS, A_HEADS)),
        'w_out': n(14, (DEPTH, MIX_WIDTH, D_MODEL)) * MIX_WIDTH ** -0.5,
        'norm2_g': 1.0 + 0.02 * n(15, (DEPTH, D_MODEL)),
        'w_query': n(16, (DEPTH, D_MODEL, PEER_HEADS * PEER_QDIM)) * D_MODEL ** -0.5,
        'sub_keys1': n(17, (DEPTH, PEER_HEADS, N_KEYS, PEER_QDIM // 2)) * (PEER_QDIM // 2) ** -0.5,
        'sub_keys2': n(18, (DEPTH, PEER_HEADS, N_KEYS, PEER_QDIM // 2)) * (PEER_QDIM // 2) ** -0.5,
        'expert_u': n(19, (DEPTH, N_EXPERTS, D_MODEL)) * D_MODEL ** -0.5,
        'expert_v': n(20, (DEPTH, N_EXPERTS, D_MODEL)) * D_MODEL ** -0.5,
    }


def reference(x, norm1_g, w_in, conv_w, conv_b, wq_m, wk_m, ig_b, fg_b, mh_norm_g, skip_m,
              qn_g, kn_g, rel_bias, w_out, norm2_g, w_query, sub_keys1, sub_keys2,
              expert_u, expert_v):
    B, S, D = x.shape
    offsets = []
    acc = 0
    for sz in IN_SIZES[:-1]:
        acc += sz
        offsets.append(acc)
    for l in range(DEPTH):
        h = rms_norm(x, norm1_g[l])
        proj = h @ w_in[l]
        u, vm, z, ipre, fpre, qa, ka, va = jnp.split(proj, offsets, axis=-1)
        c = jax.nn.silu(causal_short_conv(u, conv_w[l], conv_b[l]))
        ch = c.reshape(B, S, M_HEADS, M_HEAD_DIM)
        qm = jnp.einsum('bshc,hcd->bhsd', ch, wq_m[l]).astype(jnp.float32)
        km = jnp.einsum('bshc,hcd->bhsd', ch, wk_m[l]).astype(jnp.float32) * (M_HEAD_DIM ** -0.5)
        vmh = vm.reshape(B, S, M_HEADS, M_HEAD_DIM).transpose(0, 2, 1, 3).astype(jnp.float32)
        ig = (ipre + ig_b[l]).astype(jnp.float32).transpose(0, 2, 1)
        fg = (fpre + fg_b[l]).astype(jnp.float32).transpose(0, 2, 1)
        hm = mlstm_chunkwise(qm, km, vmh, ig, fg).transpose(0, 2, 1, 3).astype(x.dtype)
        hm = rms_norm(hm, mh_norm_g[l]) + skip_m[l] * ch
        ym = jax.nn.sigmoid(z) * hm.reshape(B, S, M_WIDTH)
        qh = rms_norm(qa.reshape(B, S, A_HEADS, A_HEAD_DIM), qn_g[l])
        kh = rms_norm(ka.reshape(B, S, A_HEADS, A_HEAD_DIM), kn_g[l])
        vh = va.reshape(B, S, A_HEADS, A_HEAD_DIM)
        outs, lses = [], []
        for window, dilation in DIL_PATTERNS:
            o, lse = dilated_branch(qh, kh, vh, rel_bias, window, dilation)
            outs.append(o)
            lses.append(lse)
        wts = jax.nn.softmax(jnp.stack(lses, axis=0), axis=0)
        ya = jnp.sum(wts[..., None] * jnp.stack(outs, axis=0), axis=0)
        ya = ya.reshape(B, S, A_WIDTH).astype(x.dtype)
        x = x + jnp.concatenate([ym, ya], axis=-1) @ w_out[l]
        h2 = rms_norm(x, norm2_g[l])
        x = x + peer(h2, w_query[l], sub_keys1[l], sub_keys2[l], expert_u[l], expert_v[l])
    return x
```

```python
import functools
import math

import numpy as np
import jax
import jax.numpy as jnp
from jax import lax
from jax.experimental import pallas as pl
from jax.experimental.pallas import tpu as pltpu

EPS = 1e-6
M_HEADS = 4
M_HEAD_DIM = 128
M_WIDTH = M_HEADS * M_HEAD_DIM
CONV_W = 4
CHUNK = 128
A_HEADS = 8
A_HEAD_DIM = 64
A_WIDTH = A_HEADS * A_HEAD_DIM
DIL_PATTERNS = ((128, 1), (512, 4), (2048, 16))
N_BUCKETS = 32
MAX_DISTANCE = 2048
PEER_HEADS = 8
N_KEYS = 128
PEER_QDIM = 256
PEER_TOPK = 16

LANES = 128
SUBLANES = 8
VMEM_LIMIT = 56 * 1024 * 1024

BF16 = jnp.bfloat16
F32 = jnp.float32
HI = lax.Precision.HIGHEST


def _cparams(sem):
    return pltpu.CompilerParams(dimension_semantics=sem, vmem_limit_bytes=VMEM_LIMIT)


def _dot(a, b):
    return jnp.dot(a, b, preferred_element_type=F32)


def _dot_nt(a, b):
    return lax.dot_general(a, b, (((1,), (1,)), ((), ())), preferred_element_type=F32)


def _dot_tn(a, b):
    return lax.dot_general(a, b, (((0,), (0,)), ((), ())), preferred_element_type=F32)


def _inproj_kernel(x_ref, g1_ref, wm_ref, wa_ref, wgt_ref, gb_ref, bd_ref, qkg_ref,
                   uvz_ref, q_ref, k_ref, v_ref, grow_ref):
    x = x_ref[...]
    h = x * lax.rsqrt(jnp.mean(x * x, axis=-1, keepdims=True) + EPS) * g1_ref[...]
    hb = h.astype(BF16)
    uvz_ref[...] = _dot(hb, wm_ref[...])
    qkv = _dot(hb, wa_ref[...])
    bd = bd_ref[...]

    def head_norm(t, g):
        sq = t * t
        hi = sq.astype(BF16)
        lo = (sq - hi.astype(F32)).astype(BF16)
        ms = (_dot(hi, bd) + _dot(lo, bd)) * (1.0 / A_HEAD_DIM)
        return t * lax.rsqrt(ms + EPS) * g

    q_ref[...] = head_norm(qkv[:, :A_WIDTH], qkg_ref[0:1, :])
    k_ref[...] = head_norm(qkv[:, A_WIDTH:2 * A_WIDTH], qkg_ref[1:2, :])
    v_ref[...] = qkv[:, 2 * A_WIDTH:]
    gr = _dot_nt(wgt_ref[...], hb) + gb_ref[...]
    lf = jnp.minimum(gr, 0.0) - jnp.log1p(jnp.exp(-jnp.abs(gr)))
    row = lax.broadcasted_iota(jnp.int32, gr.shape, 0)
    grow_ref[...] = jnp.where(row < M_HEADS, gr, lf)


def _inproj(x2d, g1, wm, wa, wgt, gb, bd, qkg, tm):
    T, D = x2d.shape
    full = lambda a: pl.BlockSpec(a.shape, lambda i: (0,) * a.ndim)
    return pl.pallas_call(
        _inproj_kernel,
        grid=(T // tm,),
        in_specs=[pl.BlockSpec((tm, D), lambda i: (i, 0)), full(g1), full(wm), full(wa), full(wgt),
                  full(gb), full(bd), full(qkg)],
        out_specs=[pl.BlockSpec((tm, 3 * M_WIDTH), lambda i: (i, 0)),
                   pl.BlockSpec((tm, A_WIDTH), lambda i: (i, 0)),
                   pl.BlockSpec((tm, A_WIDTH), lambda i: (i, 0)),
                   pl.BlockSpec((tm, A_WIDTH), lambda i: (i, 0)),
                   pl.BlockSpec((2 * M_HEADS, tm), lambda i: (0, i))],
        out_shape=[jax.ShapeDtypeStruct((T, 3 * M_WIDTH), F32),
                   jax.ShapeDtypeStruct((T, A_WIDTH), F32),
                   jax.ShapeDtypeStruct((T, A_WIDTH), F32),
                   jax.ShapeDtypeStruct((T, A_WIDTH), F32),
                   jax.ShapeDtypeStruct((2 * M_HEADS, T), F32)],
        compiler_params=_cparams(("parallel",)),
        name="inproj",
    )(x2d, g1, wm, wa, wgt, gb, bd, qkg)


def _mlstm_kernel(uvz_ref, grow_ref, cw_ref, cb_ref, wq_ref, wk_ref, ng_ref, sk_ref,
                  ym_ref, ubuf_ref, cst_ref, mst_ref):
    L = CHUNK
    c_idx = pl.program_id(1)

    @pl.when(c_idx == 0)
    def _():
        ubuf_ref[0:SUBLANES, :] = jnp.zeros((SUBLANES, M_WIDTH), F32)
        cst_ref[...] = jnp.zeros_like(cst_ref)
        mst_ref[...] = jnp.zeros_like(mst_ref)

    u = uvz_ref[:, 0:M_WIDTH]
    ubuf_ref[SUBLANES:SUBLANES + L, :] = u
    conv = cb_ref[...] + jnp.zeros((L, M_WIDTH), F32)
    for w in range(CONV_W):
        conv = conv + ubuf_ref[pl.ds(SUBLANES - (CONV_W - 1) + w, L), :] * cw_ref[w:w + 1, :]
    ubuf_ref[0:SUBLANES, :] = u[L - SUBLANES:, :]
    cact = conv * jax.nn.sigmoid(conv)

    gr = grow_ref[...]
    rr = lax.broadcasted_iota(jnp.int32, (L, L), 0)
    cc = lax.broadcasted_iota(jnp.int32, (L, L), 1)
    causal = cc <= rr
    tri_u = (rr <= cc).astype(F32)
    b_rows = jnp.dot(gr, tri_u, preferred_element_type=F32, precision=HI)
    gcol = jnp.transpose(gr)
    bcols = jnp.dot(causal.astype(F32), gcol, preferred_element_type=F32, precision=HI)

    for h in range(M_HEADS):
        lo, hi_ = h * M_HEAD_DIM, (h + 1) * M_HEAD_DIM
        ch = cact[:, lo:hi_]
        chb = ch.astype(BF16)
        q = _dot(chb, wq_ref[h])
        k = _dot(chb, wk_ref[h]) * (M_HEAD_DIM ** -0.5)
        v = uvz_ref[:, M_WIDTH + lo:M_WIDTH + hi_]
        z = uvz_ref[:, 2 * M_WIDTH + lo:2 * M_WIDTH + hi_]
        qb, kb = q.astype(BF16), k.astype(BF16)
        vaug = jnp.concatenate([v, jnp.ones((L, M_HEAD_DIM), F32)], axis=1).astype(BF16)

        b_r = b_rows[M_HEADS + h:M_HEADS + h + 1, :]
        b_c = bcols[:, M_HEADS + h:M_HEADS + h + 1]
        li_r = gr[h:h + 1, :]
        li_c = gcol[:, h:h + 1]
        b_last = b_r[:, L - 1:L]
        m_prev = mst_ref[h][0:1, 0:1]
        caug = cst_ref[h]

        log_d = b_c - b_r + li_r
        inter = b_c + m_prev
        m_t = jnp.maximum(inter, jnp.max(jnp.where(causal, log_d, -jnp.inf), axis=1, keepdims=True))
        d_mat = jnp.where(causal, jnp.exp(log_d - m_t), 0.0)
        inter_w = jnp.exp(inter - m_t)
        s = _dot_nt(qb, kb) * d_mat
        nd = _dot(s.astype(BF16), vaug) + inter_w * _dot(qb, caug.astype(BF16))
        num = nd[:, :M_HEAD_DIM]
        den = nd[:, M_HEAD_DIM:]
        hval = num / jnp.maximum(jnp.abs(den), jnp.exp(-m_t))

        g_c = b_last - b_c + li_c
        m_loc = jnp.max(g_c, axis=0, keepdims=True)
        kw = (jnp.exp(g_c - m_loc) * k).astype(BF16)
        c_new = _dot_tn(kw, vaug)
        m_new = jnp.maximum(b_last + m_prev, m_loc)
        a = jnp.exp(b_last + m_prev - m_new)
        bb = jnp.exp(m_loc - m_new)
        cst_ref[h] = a * caug + bb * c_new
        mst_ref[h] = jnp.broadcast_to(m_new, (SUBLANES, LANES))

        hn = hval * lax.rsqrt(jnp.mean(hval * hval, axis=-1, keepdims=True) + EPS) * ng_ref[:, lo:hi_]
        hn = hn + sk_ref[:, lo:hi_] * ch
        ym_ref[:, lo:hi_] = jax.nn.sigmoid(z) * hn


def _mlstm(uvz, grow, cw, cb, wq, wk, ng, sk, B, S):
    T = B * S
    nc = S // CHUNK
    full = lambda a: pl.BlockSpec(a.shape, lambda b, c: (0,) * a.ndim)
    return pl.pallas_call(
        _mlstm_kernel,
        grid=(B, nc),
        in_specs=[pl.BlockSpec((CHUNK, 3 * M_WIDTH), lambda b, c: (b * nc + c, 0)),
                  pl.BlockSpec((2 * M_HEADS, CHUNK), lambda b, c: (0, b * nc + c)),
                  full(cw), full(cb), full(wq), full(wk), full(ng), full(sk)],
        out_specs=pl.BlockSpec((CHUNK, M_WIDTH), lambda b, c: (b * nc + c, 0)),
        out_shape=jax.ShapeDtypeStruct((T, M_WIDTH), F32),
        scratch_shapes=[pltpu.VMEM((SUBLANES + CHUNK, M_WIDTH), F32),
                        pltpu.VMEM((M_HEADS, M_HEAD_DIM, 2 * M_HEAD_DIM), F32),
                        pltpu.VMEM((M_HEADS, SUBLANES, LANES), F32)],
        compiler_params=_cparams(("arbitrary", "arbitrary")),
        name="mlstm",
    )(uvz, grow, cw, cb, wq, wk, ng, sk)


def _dilattn_kernel(q_ref, kp_ref, kc_ref, vp_ref, vc_ref, bias_ref, o_ref, lse_ref):
    blk = q_ref.shape[0]
    n = pl.program_id(2)
    ii = lax.broadcasted_iota(jnp.int32, (blk, 2 * blk), 0)
    jj = lax.broadcasted_iota(jnp.int32, (blk, 2 * blk), 1)
    steps = ii + blk - jj
    valid = (steps >= 0) & (steps <= blk) & ((jj >= blk) | (n > 0))
    lane = lax.broadcasted_iota(jnp.int32, (2 * blk, LANES), 1)
    lane_q = lax.broadcasted_iota(jnp.int32, (blk, LANES), 1)
    for hp in range(A_HEADS // 2):
        lo, hi_ = hp * LANES, (hp + 1) * LANES
        qb = q_ref[:, lo:hi_].astype(BF16)
        kk = jnp.concatenate([kp_ref[:, lo:hi_], kc_ref[:, lo:hi_]], axis=0)
        vv = jnp.concatenate([vp_ref[:, lo:hi_], vc_ref[:, lo:hi_]], axis=0).astype(BF16)
        outs, lses = [], []
        for sub in range(2):
            h = 2 * hp + sub
            in_head = (lane >= sub * A_HEAD_DIM) & (lane < (sub + 1) * A_HEAD_DIM)
            kh = jnp.where(in_head, kk, 0.0).astype(BF16)
            s = _dot_nt(qb, kh) * (A_HEAD_DIM ** -0.5) + bias_ref[h]
            m = jnp.max(jnp.where(valid, s, -jnp.inf), axis=1, keepdims=True)
            p = jnp.where(valid, jnp.exp(s - m), 0.0)
            den = jnp.sum(p, axis=1, keepdims=True)
            outs.append(_dot(p.astype(BF16), vv) / den)
            lses.append(m + jnp.log(den))
        first = lane_q < A_HEAD_DIM
        o_ref[:, lo:hi_] = jnp.where(first, outs[0], outs[1])
        lse_ref[:, lo:hi_] = jnp.where(first, lses[0], lses[1])


def _dilattn(q, k, v, bias, B, S, window, dilation):
    blk = window // dilation
    assert blk == LANES and S % window == 0
    rows = S // dilation
    nb = rows // blk
    view = lambda t: t.reshape(B, rows, dilation * A_WIDTH)
    cur = pl.BlockSpec((None, blk, A_WIDTH), lambda b, r, n: (b, n, r))
    prev = pl.BlockSpec((None, blk, A_WIDTH), lambda b, r, n: (b, jnp.maximum(n - 1, 0), r))
    o, lse = pl.pallas_call(
        _dilattn_kernel,
        grid=(B, dilation, nb),
        in_specs=[cur, prev, cur, prev, cur, pl.BlockSpec(bias.shape, lambda b, r, n: (0, 0, 0))],
        out_specs=[cur, cur],
        out_shape=[jax.ShapeDtypeStruct((B, rows, dilation * A_WIDTH), F32)] * 2,
        compiler_params=_cparams(("parallel", "parallel", "arbitrary")),
        name=f"dilattn_d{dilation}",
    )(view(q), view(k), view(k), view(v), view(v), bias)
    return o.reshape(B * S, A_WIDTH), lse.reshape(B * S, A_WIDTH)


def _t5_bucket(dist):
    max_exact = N_BUCKETS // 2
    nf = jnp.maximum(dist, 1).astype(F32)
    large = max_exact + (jnp.log(nf / max_exact) / np.log(MAX_DISTANCE / max_exact)
                         * (N_BUCKETS - max_exact)).astype(jnp.int32)
    large = jnp.minimum(large, N_BUCKETS - 1)
    return jnp.where(dist < max_exact, dist, large)


def _pattern_bias(rel_bias, blk, dilation):
    i = jnp.arange(blk)[:, None]
    j = jnp.arange(2 * blk)[None, :]
    steps = i + blk - j
    return rel_bias[_t5_bucket(jnp.maximum(steps, 0) * dilation)].transpose(2, 0, 1)


def _outproj_kernel(x_ref, ym_ref, o1_ref, l1_ref, o2_ref, l2_ref, o3_ref, l3_ref, wo_ref, g2_ref,
                    x2_ref, h2t_ref):
    l1, l2, l3 = l1_ref[...], l2_ref[...], l3_ref[...]
    mx = jnp.maximum(jnp.maximum(l1, l2), l3)
    e1, e2, e3 = jnp.exp(l1 - mx), jnp.exp(l2 - mx), jnp.exp(l3 - mx)
    ya = (e1 * o1_ref[...] + e2 * o2_ref[...] + e3 * o3_ref[...]) / (e1 + e2 + e3)
    y = _dot(ym_ref[...].astype(BF16), wo_ref[0:M_WIDTH, :]) + _dot(ya.astype(BF16), wo_ref[M_WIDTH:, :])
    x2 = x_ref[...] + y
    x2_ref[...] = x2
    h2 = x2 * lax.rsqrt(jnp.mean(x2 * x2, axis=-1, keepdims=True) + EPS) * g2_ref[...]
    h2t_ref[...] = jnp.transpose(h2).astype(BF16)


def _outproj(x2d, ym, o1, l1, o2, l2, o3, l3, wo, g2, tm):
    T, D = x2d.shape
    half = pl.BlockSpec((tm, M_WIDTH), lambda i: (i, 0))
    full = lambda a: pl.BlockSpec(a.shape, lambda i: (0,) * a.ndim)
    return pl.pallas_call(
        _outproj_kernel,
        grid=(T // tm,),
        in_specs=[pl.BlockSpec((tm, D), lambda i: (i, 0))] + [half] * 7 + [full(wo), full(g2)],
        out_specs=[pl.BlockSpec((tm, D), lambda i: (i, 0)), pl.BlockSpec((D, tm), lambda i: (0, i))],
        out_shape=[jax.ShapeDtypeStruct((T, D), F32), jax.ShapeDtypeStruct((D, T), BF16)],
        compiler_params=_cparams(("parallel",)),
        name="outproj",
    )(x2d, ym, o1, l1, o2, l2, o3, l3, wo, g2)


def _peer_keys_kernel(keys_ref, wqt_ref, out_ref):
    out_ref[...] = jnp.dot(keys_ref[...], wqt_ref[...], preferred_element_type=F32,
                           precision=HI).astype(out_ref.dtype)


def _peer_keys(keys, wqt):
    _, H, K, C = keys.shape
    D = wqt.shape[-1]
    return pl.pallas_call(
        _peer_keys_kernel,
        grid=(2, H),
        in_specs=[pl.BlockSpec((None, None, K, C), lambda a, h: (a, h, 0, 0)),
                  pl.BlockSpec((None, None, C, D), lambda a, h: (a, h, 0, 0))],
        out_specs=pl.BlockSpec((None, None, K, D), lambda a, h: (a, h, 0, 0)),
        out_shape=jax.ShapeDtypeStruct((2, H, K, D), BF16),
        compiler_params=_cparams(("parallel", "parallel")),
        name="peer_keys",
    )(keys, wqt)


def _oddeven_merge_sort_pairs(n):
    pairs = []

    def merge(lo, hi, r):
        step = r * 2
        if step < hi - lo:
            merge(lo, hi, step)
            merge(lo + r, hi, step)
            for i in range(lo + r, hi - r, step):
                pairs.append((i, i + r))
        else:
            pairs.append((lo, lo + r))

    def sort(lo, hi):
        if hi - lo >= 1:
            mid = lo + (hi - lo) // 2
            sort(lo, mid)
            sort(mid + 1, hi)
            merge(lo, hi, 1)

    sort(0, n - 1)
    return pairs


_SORT16 = _oddeven_merge_sort_pairs(PEER_TOPK)


def _sort_desc(w):
    w = list(w)
    for a, b in _SORT16:
        hi, lo = jnp.maximum(w[a], w[b]), jnp.minimum(w[a], w[b])
        w[a], w[b] = hi, lo
    return w


def _merge_top(a, b):
    n = len(a)
    w = [jnp.maximum(a[i], b[n - 1 - i]) for i in range(n)]
    half = n // 2
    while half >= 1:
        for start in range(0, n, 2 * half):
            for i in range(start, start + half):
                hi, lo = jnp.maximum(w[i], w[i + half]), jnp.minimum(w[i], w[i + half])
                w[i], w[i + half] = hi, lo
        half //= 2
    return w


def _peer_topk_kernel(wk_ref, h2t_ref, u_ref, v_ref, th_ref, sc_ref):
    H, K = PEER_HEADS, N_KEYS
    tb = h2t_ref.shape[1]
    h2t = h2t_ref[...]
    for a in range(2):
        for h in range(H):
            sc_ref[a, h] = _dot(wk_ref[a, h], h2t)

    sub = lax.broadcasted_iota(jnp.int32, (SUBLANES, LANES), 0)
    margin = 1.0 - 16.0 * jnp.finfo(F32).eps
    for lt in range(tb // LANES):
        ls = slice(lt * LANES, (lt + 1) * LANES)
        packed = []
        for a in range(2):
            acc = [None] * PEER_TOPK
            for h in range(H):
                w = [sc_ref[a, h, v * SUBLANES:(v + 1) * SUBLANES, ls] for v in range(K // SUBLANES)]
                w = _sort_desc(w)
                for shift in (4, 2, 1):
                    w = _merge_top(w, [pltpu.roll(x, shift, 0) for x in w])
                for i in range(PEER_TOPK):
                    acc[i] = w[i] if h == 0 else jnp.where(sub == h, w[i], acc[i])
            packed.append(acc)
        v1, v2 = packed
        rows = [[v1[a] + v2[b] for b in range(PEER_TOPK // (a + 1))] for a in range(PEER_TOPK)]
        neg = jnp.full((SUBLANES, LANES), -jnp.inf, F32)
        l0 = rows[0]
        l1 = _sort_desc(rows[1] + rows[2] + rows[4])
        l2 = _sort_desc(rows[3] + rows[5] + rows[6] + rows[7] + [rows[a][0] for a in range(8, 14)])
        l3 = [rows[14][0], rows[15][0]]
        l3 = [jnp.maximum(l3[0], l3[1]), jnp.minimum(l3[0], l3[1])] + [neg] * (PEER_TOPK - 2)
        vc = _merge_top(_merge_top(l0, l1), _merge_top(l2, l3))
        top = vc[0]
        zsum = jnp.zeros((SUBLANES, LANES), F32)
        for i in range(PEER_TOPK):
            zsum = zsum + jnp.exp(vc[i] - top)
        inv_z = 1.0 / zsum
        th_ref[:, ls] = jnp.maximum(jnp.exp(vc[PEER_TOPK - 1] - top) * inv_z * margin, 1e-30)
        m1, m2 = v1[0], v2[0]
        for h in range(H):
            m1h = jnp.broadcast_to(m1[h:h + 1, :], (K, LANES))
            m2h = jnp.broadcast_to(m2[h:h + 1, :], (K, LANES))
            izh = jnp.broadcast_to(inv_z[h:h + 1, :], (K, LANES))
            u_ref[h, :, ls] = jnp.exp(sc_ref[0, h, :, ls] - m1h) * izh
            v_ref[h, :, ls] = jnp.exp(sc_ref[1, h, :, ls] - m2h)


def _peer_topk(wk, h2t, tb):
    D, T = h2t.shape
    H, K = PEER_HEADS, N_KEYS
    return pl.pallas_call(
        _peer_topk_kernel,
        grid=(T // tb,),
        in_specs=[pl.BlockSpec(wk.shape, lambda i: (0, 0, 0, 0)), pl.BlockSpec((D, tb), lambda i: (0, i))],
        out_specs=[pl.BlockSpec((H, K, tb), lambda i: (0, 0, i)),
                   pl.BlockSpec((H, K, tb), lambda i: (0, 0, i)),
                   pl.BlockSpec((H, tb), lambda i: (0, i))],
        out_shape=[jax.ShapeDtypeStruct((H, K, T), F32), jax.ShapeDtypeStruct((H, K, T), F32),
                   jax.ShapeDtypeStruct((H, T), F32)],
        scratch_shapes=[pltpu.VMEM((2, H, K, tb), F32)],
        compiler_params=_cparams(("parallel",)),
        name="peer_topk",
    )(wk, h2t)


def _peer_main_kernel(h2t_ref, ue_ref, vet_ref, u_ref, v_ref, th_ref, x2_ref, out_ref, acc_ref, a_ref):
    c = pl.program_id(1)
    n_i = ue_ref.shape[0] // N_KEYS

    @pl.when(c == 0)
    def _():
        acc_ref[...] = jnp.zeros_like(acc_ref)

    h2t = h2t_ref[...]
    for ii in range(n_i):
        rs = slice(ii * N_KEYS, (ii + 1) * N_KEYS)
        pre = _dot(ue_ref[rs, :], h2t)
        act = 0.5 * pre * (1.0 + lax.erf(pre * (2.0 ** -0.5)))
        gate = jnp.zeros_like(pre)
        for h in range(PEER_HEADS):
            y = u_ref[h, ii:ii + 1, :] * v_ref[h]
            gate = gate + jnp.where(y >= th_ref[h:h + 1, :], y, 0.0)
        a_ref[rs, :] = (act * gate).astype(BF16)
    acc_ref[...] += _dot(vet_ref[...], a_ref[...])

    @pl.when(c == pl.num_programs(1) - 1)
    def _():
        out_ref[...] = x2_ref[...] + jnp.transpose(acc_ref[...])


def _peer_main(h2t, ue, vet, u, v, th, x2, tb, ec):
    D, T = h2t.shape
    E = ue.shape[0]
    H, K = PEER_HEADS, N_KEYS
    return pl.pallas_call(
        _peer_main_kernel,
        grid=(T // tb, E // ec),
        in_specs=[pl.BlockSpec((D, tb), lambda t, c: (0, t)),
                  pl.BlockSpec((ec, D), lambda t, c: (c, 0)),
                  pl.BlockSpec((D, ec), lambda t, c: (0, c)),
                  pl.BlockSpec((H, ec // K, tb), lambda t, c: (0, c, t)),
                  pl.BlockSpec((H, K, tb), lambda t, c: (0, 0, t)),
                  pl.BlockSpec((H, tb), lambda t, c: (0, t)),
                  pl.BlockSpec((tb, D), lambda t, c: (t, 0))],
        out_specs=pl.BlockSpec((tb, D), lambda t, c: (t, 0)),
        out_shape=jax.ShapeDtypeStruct((T, D), F32),
        scratch_shapes=[pltpu.VMEM((D, tb), F32), pltpu.VMEM((ec, tb), BF16)],
        compiler_params=_cparams(("parallel", "arbitrary")),
        name="peer_main",
    )(h2t, ue, vet, u, v, th, x2)


def _block_rows(T, want):
    tm = min(T, want)
    assert T % tm == 0
    return tm


def _layer(x2d, B, S, norm1_g, w_in, conv_w, conv_b, wq_m, wk_m, ig_b, fg_b, mh_norm_g, skip_m,
           qn_g, kn_g, rel_bias, w_out, norm2_g, w_query, sub_keys1, sub_keys2, expert_u, expert_v):
    T, D = x2d.shape
    o_vm, o_z, o_i, o_f, o_q = M_WIDTH, 2 * M_WIDTH, 3 * M_WIDTH, 3 * M_WIDTH + M_HEADS, 3 * M_WIDTH + 2 * M_HEADS
    wm = w_in[:, :o_i].astype(BF16)
    wa = w_in[:, o_q:].astype(BF16)
    wgt = jnp.transpose(w_in[:, o_i:o_q]).astype(BF16)
    gb = jnp.concatenate([ig_b, fg_b])[:, None]
    seg = np.arange(A_WIDTH) // A_HEAD_DIM
    bd = jnp.asarray(seg[:, None] == seg[None, :], BF16)
    qkg = jnp.stack([qn_g.reshape(-1), kn_g.reshape(-1)])
    tm = _block_rows(T, 512)
    uvz, qn, kn, va, grow = _inproj(x2d, norm1_g[None, :], wm, wa, wgt, gb, bd, qkg, tm)

    ym = _mlstm(uvz, grow, conv_w, conv_b[None, :], wq_m.astype(BF16), wk_m.astype(BF16),
                mh_norm_g.reshape(1, -1), skip_m.reshape(1, -1), B, S)

    merged = []
    for window, dilation in DIL_PATTERNS:
        bias = _pattern_bias(rel_bias, window // dilation, dilation)
        merged.extend(_dilattn(qn, kn, va, bias, B, S, window, dilation))

    x2, h2t = _outproj(x2d, ym, *merged, w_out.astype(BF16), norm2_g[None, :], tm)

    keys = jnp.stack([sub_keys1, sub_keys2])
    wqt = jnp.transpose(w_query.reshape(D, PEER_HEADS, 2, PEER_QDIM // 2), (2, 1, 3, 0))
    wk = _peer_keys(keys, wqt)
    u, v, th = _peer_topk(wk, h2t, _block_rows(T, 256))
    ue = expert_u.astype(BF16)
    vet = jnp.transpose(expert_v).astype(BF16)
    return _peer_main(h2t, ue, vet, u, v, th, x2, _block_rows(T, 512), 1024)


def kernel(x, norm1_g, w_in, conv_w, conv_b, wq_m, wk_m, ig_b, fg_b, mh_norm_g, skip_m, qn_g, kn_g,
           rel_bias, w_out, norm2_g, w_query, sub_keys1, sub_keys2, expert_u, expert_v):
    B, S, D = x.shape
    assert S % DIL_PATTERNS[-1][0] == 0 and S % CHUNK == 0
    x2d = x.reshape(B * S, D)
    for l in range(norm1_g.shape[0]):
        x2d = _layer(x2d, B, S, norm1_g[l], w_in[l], conv_w[l], conv_b[l], wq_m[l], wk_m[l], ig_b[l],
                     fg_b[l], mh_norm_g[l], skip_m[l], qn_g[l], kn_g[l], rel_bias, w_out[l], norm2_g[l],
                     w_query[l], sub_keys1[l], sub_keys2[l], expert_u[l], expert_v[l])
    return x2d.reshape(B, S, D)
```

```python
import functools
import math

import numpy as np
import jax
import jax.numpy as jnp
from jax import lax
from jax.experimental import pallas as pl
from jax.experimental.pallas import tpu as pltpu

EPS = 1e-6
M_HEADS = 4
M_HEAD_DIM = 128
M_WIDTH = M_HEADS * M_HEAD_DIM
CONV_W = 4
CHUNK = 128
A_HEADS = 8
A_HEAD_DIM = 64
A_WIDTH = A_HEADS * A_HEAD_DIM
DIL_PATTERNS = ((128, 1), (512, 4), (2048, 16))
N_BUCKETS = 32
MAX_DISTANCE = 2048
PEER_HEADS = 8
N_KEYS = 128
PEER_QDIM = 256
PEER_TOPK = 16

LANES = 128
SUBLANES = 8
VMEM_LIMIT = 56 * 1024 * 1024

BF16 = jnp.bfloat16
F32 = jnp.float32
HI = lax.Precision.HIGHEST


def _cparams(sem):
    return pltpu.CompilerParams(dimension_semantics=sem, vmem_limit_bytes=VMEM_LIMIT)


def _dot(a, b):
    return jnp.dot(a, b, preferred_element_type=F32)


def _dot_nt(a, b):
    return lax.dot_general(a, b, (((1,), (1,)), ((), ())), preferred_element_type=F32)


def _dot_tn(a, b):
    return lax.dot_general(a, b, (((0,), (0,)), ((), ())), preferred_element_type=F32)


def _inproj_kernel(x_ref, g1_ref, wm_ref, wa_ref, wgt_ref, gb_ref, bd_ref, qkg_ref,
                   uvz_ref, q_ref, k_ref, v_ref, grow_ref):
    x = x_ref[...]
    h = x * lax.rsqrt(jnp.mean(x * x, axis=-1, keepdims=True) + EPS) * g1_ref[...]
    hb = h.astype(BF16)
    uvz_ref[...] = _dot(hb, wm_ref[...])
    qkv = _dot(hb, wa_ref[...])
    bd = bd_ref[...]

    def head_norm(t, g):
        sq = t * t
        hi = sq.astype(BF16)
        lo = (sq - hi.astype(F32)).astype(BF16)
        ms = (_dot(hi, bd) + _dot(lo, bd)) * (1.0 / A_HEAD_DIM)
        return t * lax.rsqrt(ms + EPS) * g

    q_ref[...] = head_norm(qkv[:, :A_WIDTH], qkg_ref[0:1, :])
    k_ref[...] = head_norm(qkv[:, A_WIDTH:2 * A_WIDTH], qkg_ref[1:2, :])
    v_ref[...] = qkv[:, 2 * A_WIDTH:]
    gr = _dot_nt(wgt_ref[...], hb) + gb_ref[...]
    lf = jnp.minimum(gr, 0.0) - jnp.log1p(jnp.exp(-jnp.abs(gr)))
    row = lax.broadcasted_iota(jnp.int32, gr.shape, 0)
    grow_ref[...] = jnp.where(row < M_HEADS, gr, lf)


def _inproj(x2d, g1, wm, wa, wgt, gb, bd, qkg, tm):
    T, D = x2d.shape
    full = lambda a: pl.BlockSpec(a.shape, lambda i: (0,) * a.ndim)
    return pl.pallas_call(
        _inproj_kernel,
        grid=(T // tm,),
        in_specs=[pl.BlockSpec((tm, D), lambda i: (i, 0)), full(g1), full(wm), full(wa), full(wgt),
                  full(gb), full(bd), full(qkg)],
        out_specs=[pl.BlockSpec((tm, 3 * M_WIDTH), lambda i: (i, 0)),
                   pl.BlockSpec((tm, A_WIDTH), lambda i: (i, 0)),
                   pl.BlockSpec((tm, A_WIDTH), lambda i: (i, 0)),
                   pl.BlockSpec((tm, A_WIDTH), lambda i: (i, 0)),
                   pl.BlockSpec((2 * M_HEADS, tm), lambda i: (0, i))],
        out_shape=[jax.ShapeDtypeStruct((T, 3 * M_WIDTH), F32),
                   jax.ShapeDtypeStruct((T, A_WIDTH), F32),
                   jax.ShapeDtypeStruct((T, A_WIDTH), F32),
                   jax.ShapeDtypeStruct((T, A_WIDTH), F32),
                   jax.ShapeDtypeStruct((2 * M_HEADS, T), F32)],
        compiler_params=_cparams(("parallel",)),
        name="inproj",
    )(x2d, g1, wm, wa, wgt, gb, bd, qkg)


def _mlstm_kernel(uvz_ref, grow_ref, cw_ref, cb_ref, wq_ref, wk_ref, ng_ref, sk_ref,
                  ym_ref, ubuf_ref, cst_ref, mst_ref):
    L = CHUNK
    c_idx = pl.program_id(1)

    @pl.when(c_idx == 0)
    def _():
        ubuf_ref[0:SUBLANES, :] = jnp.zeros((SUBLANES, M_WIDTH), F32)
        cst_ref[...] = jnp.zeros_like(cst_ref)
        mst_ref[...] = jnp.zeros_like(mst_ref)

    u = uvz_ref[:, 0:M_WIDTH]
    ubuf_ref[SUBLANES:SUBLANES + L, :] = u
    conv = cb_ref[...] + jnp.zeros((L, M_WIDTH), F32)
    for w in range(CONV_W):
        conv = conv + ubuf_ref[pl.ds(SUBLANES - (CONV_W - 1) + w, L), :] * cw_ref[w:w + 1, :]
    ubuf_ref[0:SUBLANES, :] = u[L - SUBLANES:, :]
    cact = conv * jax.nn.sigmoid(conv)

    gr = grow_ref[...]
    rr = lax.broadcasted_iota(jnp.int32, (L, L), 0)
    cc = lax.broadcasted_iota(jnp.int32, (L, L), 1)
    causal = cc <= rr
    tri_u = (rr <= cc).astype(F32)
    b_rows = jnp.dot(gr, tri_u, preferred_element_type=F32, precision=HI)
    gcol = jnp.transpose(gr)
    bcols = jnp.dot(causal.astype(F32), gcol, preferred_element_type=F32, precision=HI)

    for h in range(M_HEADS):
        lo, hi_ = h * M_HEAD_DIM, (h + 1) * M_HEAD_DIM
        ch = cact[:, lo:hi_]
        chb = ch.astype(BF16)
        q = _dot(chb, wq_ref[h])
        k = _dot(chb, wk_ref[h]) * (M_HEAD_DIM ** -0.5)
        v = uvz_ref[:, M_WIDTH + lo:M_WIDTH + hi_]
        z = uvz_ref[:, 2 * M_WIDTH + lo:2 * M_WIDTH + hi_]
        qb, kb = q.astype(BF16), k.astype(BF16)
        vaug = jnp.concatenate([v, jnp.ones((L, M_HEAD_DIM), F32)], axis=1).astype(BF16)

        b_r = b_rows[M_HEADS + h:M_HEADS + h + 1, :]
        b_c = bcols[:, M_HEADS + h:M_HEADS + h + 1]
        li_r = gr[h:h + 1, :]
        li_c = gcol[:, h:h + 1]
        b_last = b_r[:, L - 1:L]
        m_prev = mst_ref[h][0:1, 0:1]
        caug = cst_ref[h]

        log_d = b_c - b_r + li_r
        inter = b_c + m_prev
        m_t = jnp.maximum(inter, jnp.max(jnp.where(causal, log_d, -jnp.inf), axis=1, keepdims=True))
        d_mat = jnp.where(causal, jnp.exp(log_d - m_t), 0.0)
        inter_w = jnp.exp(inter - m_t)
        s = _dot_nt(qb, kb) * d_mat
        nd = _dot(s.astype(BF16), vaug) + inter_w * _dot(qb, caug.astype(BF16))
        num = nd[:, :M_HEAD_DIM]
        den = nd[:, M_HEAD_DIM:]
        hval = num / jnp.maximum(jnp.abs(den), jnp.exp(-m_t))

        g_c = b_last - b_c + li_c
        m_loc = jnp.max(g_c, axis=0, keepdims=True)
        kw = (jnp.exp(g_c - m_loc) * k).astype(BF16)
        c_new = _dot_tn(kw, vaug)
        m_new = jnp.maximum(b_last + m_prev, m_loc)
        a = jnp.exp(b_last + m_prev - m_new)
        bb = jnp.exp(m_loc - m_new)
        cst_ref[h] = a * caug + bb * c_new
        mst_ref[h] = jnp.broadcast_to(m_new, (SUBLANES, LANES))

        hn = hval * lax.rsqrt(jnp.mean(hval * hval, axis=-1, keepdims=True) + EPS) * ng_ref[:, lo:hi_]
        hn = hn + sk_ref[:, lo:hi_] * ch
        ym_ref[:, lo:hi_] = jax.nn.sigmoid(z) * hn


def _mlstm(uvz, grow, cw, cb, wq, wk, ng, sk, B, S):
    T = B * S
    nc = S // CHUNK
    full = lambda a: pl.BlockSpec(a.shape, lambda b, c: (0,) * a.ndim)
    return pl.pallas_call(
        _mlstm_kernel,
        grid=(B, nc),
        in_specs=[pl.BlockSpec((CHUNK, 3 * M_WIDTH), lambda b, c: (b * nc + c, 0)),
                  pl.BlockSpec((2 * M_HEADS, CHUNK), lambda b, c: (0, b * nc + c)),
                  full(cw), full(cb), full(wq), full(wk), full(ng), full(sk)],
        out_specs=pl.BlockSpec((CHUNK, M_WIDTH), lambda b, c: (b * nc + c, 0)),
        out_shape=jax.ShapeDtypeStruct((T, M_WIDTH), F32),
        scratch_shapes=[pltpu.VMEM((SUBLANES + CHUNK, M_WIDTH), F32),
                        pltpu.VMEM((M_HEADS, M_HEAD_DIM, 2 * M_HEAD_DIM), F32),
                        pltpu.VMEM((M_HEADS, SUBLANES, LANES), F32)],
        compiler_params=_cparams(("arbitrary", "arbitrary")),
        name="mlstm",
    )(uvz, grow, cw, cb, wq, wk, ng, sk)


ATT_BLK = DIL_PATTERNS[0][0] // DIL_PATTERNS[0][1]
ATT_SPAN = DIL_PATTERNS[-1][0]


def _dilattn_kernel(q_ref, kp_ref, kc_ref, vp_ref, vc_ref, bias_ref, o_ref,
                    k2_ref, v2_ref, am_ref, al_ref, ao_ref):
    blk, span = ATT_BLK, ATT_SPAN
    n = pl.program_id(1)
    k2_ref[0:span, :] = kp_ref[...]
    k2_ref[span:, :] = kc_ref[...]
    v2_ref[0:span, :] = vp_ref[...]
    v2_ref[span:, :] = vc_ref[...]
    lane_k = lax.broadcasted_iota(jnp.int32, (2 * blk, LANES), 1)
    first_q = lax.broadcasted_iota(jnp.int32, (blk, LANES), 1) < A_HEAD_DIM

    for p, (window, d) in enumerate(DIL_PATTERNS):
        units = span // blk

        def unit(u, carry, p=p, d=d):
            m_idx, r = u >> (d.bit_length() - 1), u & (d - 1)
            base = m_idx * (blk * d) + r
            if d == 1:
                base = pl.multiple_of(base, blk)
            rows = lambda off: pl.ds(off, blk) if d == 1 else pl.ds(off, blk, stride=d)
            first = jnp.logical_and(n == 0, m_idx == 0).astype(jnp.int32)
            qb = (q_ref[rows(base), :] * (A_HEAD_DIM ** -0.5)).astype(BF16)
            kk = jnp.concatenate([k2_ref[rows(span + base - blk * d), :], k2_ref[rows(span + base), :]], axis=0)
            vv = jnp.concatenate([v2_ref[rows(span + base - blk * d), :], v2_ref[rows(span + base), :]],
                                 axis=0).astype(BF16)
            ms, ls, os_ = [], [], []
            for sub in range(2):
                in_head = (lane_k >= sub * A_HEAD_DIM) & (lane_k < (sub + 1) * A_HEAD_DIM)
                kh = jnp.where(in_head, kk, 0.0).astype(BF16)
                s = _dot_nt(qb, kh) + bias_ref[p, first, sub]
                m = jnp.max(s, axis=1, keepdims=True)
                e = jnp.exp(s - m)
                ms.append(m)
                ls.append(jnp.sum(e, axis=1, keepdims=True))
                os_.append(_dot(e.astype(BF16), vv))
            m_c = jnp.where(first_q, ms[0], ms[1])
            l_c = jnp.where(first_q, ls[0], ls[1])
            o_c = jnp.where(first_q, os_[0], os_[1])
            if p == 0:
                am_ref[rows(base), :] = m_c
                al_ref[rows(base), :] = l_c
                ao_ref[rows(base), :] = o_c
            else:
                m_o = am_ref[rows(base), :]
                m_n = jnp.maximum(m_o, m_c)
                a_o, a_c = jnp.exp(m_o - m_n), jnp.exp(m_c - m_n)
                am_ref[rows(base), :] = m_n
                al_ref[rows(base), :] = a_o * al_ref[rows(base), :] + a_c * l_c
                ao_ref[rows(base), :] = a_o * ao_ref[rows(base), :] + a_c * o_c
            return carry

        lax.fori_loop(0, units, unit, 0, unroll=8)

    o_ref[...] = ao_ref[...] / al_ref[...]


def _dilattn(q, k, v, bias, B, S):
    span = ATT_SPAN
    ns = S // span
    T = B * S
    cur = pl.BlockSpec((span, LANES), lambda b, n, hp: (b * ns + n, hp))
    prev = pl.BlockSpec((span, LANES), lambda b, n, hp: (b * ns + jnp.maximum(n - 1, 0), hp))
    return pl.pallas_call(
        _dilattn_kernel,
        grid=(B, ns, A_HEADS // 2),
        in_specs=[cur, prev, cur, prev, cur,
                  pl.BlockSpec((len(DIL_PATTERNS), 2, 2, ATT_BLK, 2 * ATT_BLK), lambda b, n, hp: (0, 0, hp, 0, 0))],
        out_specs=cur,
        out_shape=jax.ShapeDtypeStruct((T, A_WIDTH), F32),
        scratch_shapes=[pltpu.VMEM((2 * span, LANES), F32), pltpu.VMEM((2 * span, LANES), F32),
                        pltpu.VMEM((span, LANES), F32), pltpu.VMEM((span, LANES), F32),
                        pltpu.VMEM((span, LANES), F32)],
        compiler_params=_cparams(("parallel", "parallel", "parallel")),
        name="dilattn",
    )(q, k, k, v, v, bias)


def _attn_bias_kernel(bkt_ref, rb_ref, out_ref):
    h = pl.program_id(1)
    bkt = bkt_ref[...]
    acc = jnp.full(bkt.shape, -jnp.inf, F32)
    for kk in range(N_BUCKETS):
        acc = jnp.where(bkt == kk, rb_ref[kk, h], acc)
    col = lax.broadcasted_iota(jnp.int32, bkt.shape, 1)
    out_ref[0] = acc
    out_ref[1] = jnp.where(col < ATT_BLK, -jnp.inf, acc)


def _attn_bias(rel_bias):
    blk = ATT_BLK
    i = jnp.arange(blk)[:, None]
    j = jnp.arange(2 * blk)[None, :]
    steps = i + blk - j
    band = (steps >= 0) & (steps <= blk)
    bkt = jnp.stack([jnp.where(band, _t5_bucket(jnp.maximum(steps, 0) * d), -1) for _, d in DIL_PATTERNS])
    P = len(DIL_PATTERNS)
    return pl.pallas_call(
        _attn_bias_kernel,
        grid=(P, A_HEADS),
        in_specs=[pl.BlockSpec((None, blk, 2 * blk), lambda p, h: (p, 0, 0)),
                  pl.BlockSpec(memory_space=pltpu.SMEM)],
        out_specs=pl.BlockSpec((None, 2, None, blk, 2 * blk), lambda p, h: (p, 0, h, 0, 0)),
        out_shape=jax.ShapeDtypeStruct((P, 2, A_HEADS, blk, 2 * blk), F32),
        compiler_params=_cparams(("parallel", "parallel")),
        name="attn_bias",
    )(bkt.astype(jnp.int32), rel_bias)


def _t5_bucket(dist):
    max_exact = N_BUCKETS // 2
    nf = jnp.maximum(dist, 1).astype(F32)
    large = max_exact + (jnp.log(nf / max_exact) / np.log(MAX_DISTANCE / max_exact)
                         * (N_BUCKETS - max_exact)).astype(jnp.int32)
    large = jnp.minimum(large, N_BUCKETS - 1)
    return jnp.where(dist < max_exact, dist, large)


def _outproj_kernel(x_ref, ym_ref, ya_ref, wo_ref, g2_ref, x2_ref, h2t_ref):
    y = (_dot(ym_ref[...].astype(BF16), wo_ref[0:M_WIDTH, :])
         + _dot(ya_ref[...].astype(BF16), wo_ref[M_WIDTH:, :]))
    x2 = x_ref[...] + y
    x2_ref[...] = x2
    h2 = x2 * lax.rsqrt(jnp.mean(x2 * x2, axis=-1, keepdims=True) + EPS) * g2_ref[...]
    h2t_ref[...] = jnp.transpose(h2).astype(BF16)


def _outproj(x2d, ym, ya, wo, g2, tm):
    T, D = x2d.shape
    half = pl.BlockSpec((tm, M_WIDTH), lambda i: (i, 0))
    full = lambda a: pl.BlockSpec(a.shape, lambda i: (0,) * a.ndim)
    return pl.pallas_call(
        _outproj_kernel,
        grid=(T // tm,),
        in_specs=[pl.BlockSpec((tm, D), lambda i: (i, 0)), half, half, full(wo), full(g2)],
        out_specs=[pl.BlockSpec((tm, D), lambda i: (i, 0)), pl.BlockSpec((D, tm), lambda i: (0, i))],
        out_shape=[jax.ShapeDtypeStruct((T, D), F32), jax.ShapeDtypeStruct((D, T), BF16)],
        compiler_params=_cparams(("parallel",)),
        name="outproj",
    )(x2d, ym, ya, wo, g2)


def _peer_keys_kernel(keys_ref, wqt_ref, out_ref):
    out_ref[...] = jnp.dot(keys_ref[...], wqt_ref[...], preferred_element_type=F32,
                           precision=HI).astype(out_ref.dtype)


def _peer_keys(keys, wqt):
    _, H, K, C = keys.shape
    D = wqt.shape[-1]
    return pl.pallas_call(
        _peer_keys_kernel,
        grid=(2, H),
        in_specs=[pl.BlockSpec((None, None, K, C), lambda a, h: (a, h, 0, 0)),
                  pl.BlockSpec((None, None, C, D), lambda a, h: (a, h, 0, 0))],
        out_specs=pl.BlockSpec((None, None, K, D), lambda a, h: (a, h, 0, 0)),
        out_shape=jax.ShapeDtypeStruct((2, H, K, D), BF16),
        compiler_params=_cparams(("parallel", "parallel")),
        name="peer_keys",
    )(keys, wqt)


def _oddeven_merge_sort_pairs(n):
    pairs = []

    def merge(lo, hi, r):
        step = r * 2
        if step < hi - lo:
            merge(lo, hi, step)
            merge(lo + r, hi, step)
            for i in range(lo + r, hi - r, step):
                pairs.append((i, i + r))
        else:
            pairs.append((lo, lo + r))

    def sort(lo, hi):
        if hi - lo >= 1:
            mid = lo + (hi - lo) // 2
            sort(lo, mid)
            sort(mid + 1, hi)
            merge(lo, hi, 1)

    sort(0, n - 1)
    return pairs


_SORT16 = _oddeven_merge_sort_pairs(PEER_TOPK)


def _sort_desc(w):
    w = list(w)
    for a, b in _SORT16:
        hi, lo = jnp.maximum(w[a], w[b]), jnp.minimum(w[a], w[b])
        w[a], w[b] = hi, lo
    return w


def _merge_top(a, b):
    n = len(a)
    w = [jnp.maximum(a[i], b[n - 1 - i]) for i in range(n)]
    half = n // 2
    while half >= 1:
        for start in range(0, n, 2 * half):
            for i in range(start, start + half):
                hi, lo = jnp.maximum(w[i], w[i + half]), jnp.minimum(w[i], w[i + half])
                w[i], w[i + half] = hi, lo
        half //= 2
    return w


def _peer_topk_kernel(wk_ref, h2t_ref, u_ref, v_ref, th_ref, sc_ref):
    H, K = PEER_HEADS, N_KEYS
    tb = h2t_ref.shape[1]
    h2t = h2t_ref[...]
    for a in range(2):
        for h in range(H):
            sc_ref[a, h] = _dot(wk_ref[a, h], h2t)

    sub = lax.broadcasted_iota(jnp.int32, (SUBLANES, LANES), 0)
    margin = 1.0 - 16.0 * jnp.finfo(F32).eps
    for lt in range(tb // LANES):
        ls = slice(lt * LANES, (lt + 1) * LANES)
        packed = []
        for a in range(2):
            acc = [None] * PEER_TOPK
            for h in range(H):
                w = [sc_ref[a, h, v * SUBLANES:(v + 1) * SUBLANES, ls] for v in range(K // SUBLANES)]
                w = _sort_desc(w)
                for shift in (4, 2, 1):
                    w = _merge_top(w, [pltpu.roll(x, shift, 0) for x in w])
                for i in range(PEER_TOPK):
                    acc[i] = w[i] if h == 0 else jnp.where(sub == h, w[i], acc[i])
            packed.append(acc)
        v1, v2 = packed
        rows = [[v1[a] + v2[b] for b in range(PEER_TOPK // (a + 1))] for a in range(PEER_TOPK)]
        neg = jnp.full((SUBLANES, LANES), -jnp.inf, F32)
        l0 = rows[0]
        l1 = _sort_desc(rows[1] + rows[2] + rows[4])
        l2 = _sort_desc(rows[3] + rows[5] + rows[6] + rows[7] + [rows[a][0] for a in range(8, 14)])
        l3 = [rows[14][0], rows[15][0]]
        l3 = [jnp.maximum(l3[0], l3[1]), jnp.minimum(l3[0], l3[1])] + [neg] * (PEER_TOPK - 2)
        vc = _merge_top(_merge_top(l0, l1), _merge_top(l2, l3))
        top = vc[0]
        zsum = jnp.zeros((SUBLANES, LANES), F32)
        for i in range(PEER_TOPK):
            zsum = zsum + jnp.exp(vc[i] - top)
        inv_z = 1.0 / zsum
        th_ref[:, ls] = jnp.maximum(jnp.exp(vc[PEER_TOPK - 1] - top) * inv_z * margin, 1e-30)
        m1, m2 = v1[0], v2[0]
        for h in range(H):
            m1h = jnp.broadcast_to(m1[h:h + 1, :], (K, LANES))
            m2h = jnp.broadcast_to(m2[h:h + 1, :], (K, LANES))
            izh = jnp.broadcast_to(inv_z[h:h + 1, :], (K, LANES))
            u_ref[h, :, ls] = jnp.exp(sc_ref[0, h, :, ls] - m1h) * izh
            v_ref[h, :, ls] = jnp.exp(sc_ref[1, h, :, ls] - m2h)


def _peer_topk(wk, h2t, tb):
    D, T = h2t.shape
    H, K = PEER_HEADS, N_KEYS
    return pl.pallas_call(
        _peer_topk_kernel,
        grid=(T // tb,),
        in_specs=[pl.BlockSpec(wk.shape, lambda i: (0, 0, 0, 0)), pl.BlockSpec((D, tb), lambda i: (0, i))],
        out_specs=[pl.BlockSpec((H, K, tb), lambda i: (0, 0, i)),
                   pl.BlockSpec((H, K, tb), lambda i: (0, 0, i)),
                   pl.BlockSpec((H, tb), lambda i: (0, i))],
        out_shape=[jax.ShapeDtypeStruct((H, K, T), F32), jax.ShapeDtypeStruct((H, K, T), F32),
                   jax.ShapeDtypeStruct((H, T), F32)],
        scratch_shapes=[pltpu.VMEM((2, H, K, tb), F32)],
        compiler_params=_cparams(("parallel",)),
        name="peer_topk",
    )(wk, h2t)


def _peer_main_kernel(h2t_ref, ue_ref, vet_ref, u_ref, v_ref, th_ref, x2_ref, out_ref, acc_ref, a_ref):
    c = pl.program_id(1)
    n_i = ue_ref.shape[0] // N_KEYS

    @pl.when(c == 0)
    def _():
        acc_ref[...] = jnp.zeros_like(acc_ref)

    h2t = h2t_ref[...]
    for ii in range(n_i):
        rs = slice(ii * N_KEYS, (ii + 1) * N_KEYS)
        pre = _dot(ue_ref[rs, :], h2t)
        act = 0.5 * pre * (1.0 + lax.erf(pre * (2.0 ** -0.5)))
        gate = jnp.zeros_like(pre)
        for h in range(PEER_HEADS):
            y = u_ref[h, ii:ii + 1, :] * v_ref[h]
            gate = gate + jnp.where(y >= th_ref[h:h + 1, :], y, 0.0)
        a_ref[rs, :] = (act * gate).astype(BF16)
    acc_ref[...] += _dot(vet_ref[...], a_ref[...])

    @pl.when(c == pl.num_programs(1) - 1)
    def _():
        out_ref[...] = x2_ref[...] + jnp.transpose(acc_ref[...])


def _peer_main(h2t, ue, vet, u, v, th, x2, tb, ec):
    D, T = h2t.shape
    E = ue.shape[0]
    H, K = PEER_HEADS, N_KEYS
    return pl.pallas_call(
        _peer_main_kernel,
        grid=(T // tb, E // ec),
        in_specs=[pl.BlockSpec((D, tb), lambda t, c: (0, t)),
                  pl.BlockSpec((ec, D), lambda t, c: (c, 0)),
                  pl.BlockSpec((D, ec), lambda t, c: (0, c)),
                  pl.BlockSpec((H, ec // K, tb), lambda t, c: (0, c, t)),
                  pl.BlockSpec((H, K, tb), lambda t, c: (0, 0, t)),
                  pl.BlockSpec((H, tb), lambda t, c: (0, t)),
                  pl.BlockSpec((tb, D), lambda t, c: (t, 0))],
        out_specs=pl.BlockSpec((tb, D), lambda t, c: (t, 0)),
        out_shape=jax.ShapeDtypeStruct((T, D), F32),
        scratch_shapes=[pltpu.VMEM((D, tb), F32), pltpu.VMEM((ec, tb), BF16)],
        compiler_params=_cparams(("parallel", "arbitrary")),
        name="peer_main",
    )(h2t, ue, vet, u, v, th, x2)


def _block_rows(T, want):
    tm = min(T, want)
    assert T % tm == 0
    return tm


def _layer(x2d, B, S, norm1_g, w_in, conv_w, conv_b, wq_m, wk_m, ig_b, fg_b, mh_norm_g, skip_m,
           qn_g, kn_g, rel_bias, w_out, norm2_g, w_query, sub_keys1, sub_keys2, expert_u, expert_v):
    T, D = x2d.shape
    o_vm, o_z, o_i, o_f, o_q = M_WIDTH, 2 * M_WIDTH, 3 * M_WIDTH, 3 * M_WIDTH + M_HEADS, 3 * M_WIDTH + 2 * M_HEADS
    wm = w_in[:, :o_i].astype(BF16)
    wa = w_in[:, o_q:].astype(BF16)
    wgt = jnp.transpose(w_in[:, o_i:o_q]).astype(BF16)
    gb = jnp.concatenate([ig_b, fg_b])[:, None]
    seg = np.arange(A_WIDTH) // A_HEAD_DIM
    bd = jnp.asarray(seg[:, None] == seg[None, :], BF16)
    qkg = jnp.stack([qn_g.reshape(-1), kn_g.reshape(-1)])
    tm = _block_rows(T, 512)
    uvz, qn, kn, va, grow = _inproj(x2d, norm1_g[None, :], wm, wa, wgt, gb, bd, qkg, tm)

    ym = _mlstm(uvz, grow, conv_w, conv_b[None, :], wq_m.astype(BF16), wk_m.astype(BF16),
                mh_norm_g.reshape(1, -1), skip_m.reshape(1, -1), B, S)

    ya = _dilattn(qn, kn, va, _attn_bias(rel_bias), B, S)

    x2, h2t = _outproj(x2d, ym, ya, w_out.astype(BF16), norm2_g[None, :], tm)

    keys = jnp.stack([sub_keys1, sub_keys2])
    wqt = jnp.transpose(w_query.reshape(D, PEER_HEADS, 2, PEER_QDIM // 2), (2, 1, 3, 0))
    wk = _peer_keys(keys, wqt)
    u, v, th = _peer_topk(wk, h2t, _block_rows(T, 256))
    ue = expert_u.astype(BF16)
    vet = jnp.transpose(expert_v).astype(BF16)
    return _peer_main(h2t, ue, vet, u, v, th, x2, _block_rows(T, 512), 1024)


def kernel(x, norm1_g, w_in, conv_w, conv_b, wq_m, wk_m, ig_b, fg_b, mh_norm_g, skip_m, qn_g, kn_g,
           rel_bias, w_out, norm2_g, w_query, sub_keys1, sub_keys2, expert_u, expert_v):
    B, S, D = x.shape
    assert S % DIL_PATTERNS[-1][0] == 0 and S % CHUNK == 0
    x2d = x.reshape(B * S, D)
    for l in range(norm1_g.shape[0]):
        x2d = _layer(x2d, B, S, norm1_g[l], w_in[l], conv_w[l], conv_b[l], wq_m[l], wk_m[l], ig_b[l],
                     fg_b[l], mh_norm_g[l], skip_m[l], qn_g[l], kn_g[l], rel_bias, w_out[l], norm2_g[l],
                     w_query[l], sub_keys1[l], sub_keys2[l], expert_u[l], expert_v[l])
    return x2d.reshape(B, S, D)
```

```python
import functools
import math

import numpy as np
import jax
import jax.numpy as jnp
from jax import lax
from jax.experimental import pallas as pl
from jax.experimental.pallas import tpu as pltpu

EPS = 1e-6
M_HEADS = 4
M_HEAD_DIM = 128
M_WIDTH = M_HEADS * M_HEAD_DIM
CONV_W = 4
CHUNK = 128
A_HEADS = 8
A_HEAD_DIM = 64
A_WIDTH = A_HEADS * A_HEAD_DIM
DIL_PATTERNS = ((128, 1), (512, 4), (2048, 16))
N_BUCKETS = 32
MAX_DISTANCE = 2048
PEER_HEADS = 8
N_KEYS = 128
PEER_QDIM = 256
PEER_TOPK = 16

LANES = 128
SUBLANES = 8
VMEM_LIMIT = 56 * 1024 * 1024

BF16 = jnp.bfloat16
F32 = jnp.float32
HI = lax.Precision.HIGHEST


def _cparams(sem):
    return pltpu.CompilerParams(dimension_semantics=sem, vmem_limit_bytes=VMEM_LIMIT)


def _dot(a, b):
    return jnp.dot(a, b, preferred_element_type=F32)


def _dot_nt(a, b):
    return lax.dot_general(a, b, (((1,), (1,)), ((), ())), preferred_element_type=F32)


def _dot_tn(a, b):
    return lax.dot_general(a, b, (((0,), (0,)), ((), ())), preferred_element_type=F32)


def _inproj_kernel(x_ref, g1_ref, wm_ref, wa_ref, wgt_ref, gb_ref, bd_ref, qkg_ref,
                   uvz_ref, q_ref, k_ref, v_ref, grow_ref):
    x = x_ref[...]
    h = x * lax.rsqrt(jnp.mean(x * x, axis=-1, keepdims=True) + EPS) * g1_ref[...]
    hb = h.astype(BF16)
    uvz_ref[...] = _dot(hb, wm_ref[...])
    qkv = _dot(hb, wa_ref[...])
    bd = bd_ref[...]

    def head_norm(t, g):
        sq = t * t
        hi = sq.astype(BF16)
        lo = (sq - hi.astype(F32)).astype(BF16)
        ms = (_dot(hi, bd) + _dot(lo, bd)) * (1.0 / A_HEAD_DIM)
        return t * lax.rsqrt(ms + EPS) * g

    q_ref[...] = head_norm(qkv[:, :A_WIDTH], qkg_ref[0:1, :])
    k_ref[...] = head_norm(qkv[:, A_WIDTH:2 * A_WIDTH], qkg_ref[1:2, :])
    v_ref[...] = qkv[:, 2 * A_WIDTH:]
    gr = _dot_nt(wgt_ref[...], hb) + gb_ref[...]
    lf = jnp.minimum(gr, 0.0) - jnp.log1p(jnp.exp(-jnp.abs(gr)))
    row = lax.broadcasted_iota(jnp.int32, gr.shape, 0)
    grow_ref[...] = jnp.where(row < M_HEADS, gr, lf)


def _inproj(x2d, g1, wm, wa, wgt, gb, bd, qkg, tm):
    T, D = x2d.shape
    full = lambda a: pl.BlockSpec(a.shape, lambda i: (0,) * a.ndim)
    return pl.pallas_call(
        _inproj_kernel,
        grid=(T // tm,),
        in_specs=[pl.BlockSpec((tm, D), lambda i: (i, 0)), full(g1), full(wm), full(wa), full(wgt),
                  full(gb), full(bd), full(qkg)],
        out_specs=[pl.BlockSpec((tm, 3 * M_WIDTH), lambda i: (i, 0)),
                   pl.BlockSpec((tm, A_WIDTH), lambda i: (i, 0)),
                   pl.BlockSpec((tm, A_WIDTH), lambda i: (i, 0)),
                   pl.BlockSpec((tm, A_WIDTH), lambda i: (i, 0)),
                   pl.BlockSpec((2 * M_HEADS, tm), lambda i: (0, i))],
        out_shape=[jax.ShapeDtypeStruct((T, 3 * M_WIDTH), F32),
                   jax.ShapeDtypeStruct((T, A_WIDTH), F32),
                   jax.ShapeDtypeStruct((T, A_WIDTH), F32),
                   jax.ShapeDtypeStruct((T, A_WIDTH), F32),
                   jax.ShapeDtypeStruct((2 * M_HEADS, T), F32)],
        compiler_params=_cparams(("parallel",)),
        name="inproj",
    )(x2d, g1, wm, wa, wgt, gb, bd, qkg)


def _mlstm_kernel(uvz_ref, grow_ref, cw_ref, cb_ref, wq_ref, wk_ref, ng_ref, sk_ref,
                  ym_ref, ubuf_ref, cst_ref, mst_ref):
    L = CHUNK
    c_idx = pl.program_id(1)

    @pl.when(c_idx == 0)
    def _():
        ubuf_ref[0:SUBLANES, :] = jnp.zeros((SUBLANES, M_WIDTH), F32)
        cst_ref[...] = jnp.zeros_like(cst_ref)
        mst_ref[...] = jnp.zeros_like(mst_ref)

    u = uvz_ref[:, 0:M_WIDTH]
    ubuf_ref[SUBLANES:SUBLANES + L, :] = u
    conv = cb_ref[...] + jnp.zeros((L, M_WIDTH), F32)
    for w in range(CONV_W):
        conv = conv + ubuf_ref[pl.ds(SUBLANES - (CONV_W - 1) + w, L), :] * cw_ref[w:w + 1, :]
    ubuf_ref[0:SUBLANES, :] = u[L - SUBLANES:, :]
    cact = conv * jax.nn.sigmoid(conv)

    gr = grow_ref[...]
    rr = lax.broadcasted_iota(jnp.int32, (L, L), 0)
    cc = lax.broadcasted_iota(jnp.int32, (L, L), 1)
    causal = cc <= rr
    tri_u = (rr <= cc).astype(F32)
    b_rows = jnp.dot(gr, tri_u, preferred_element_type=F32, precision=HI)
    gcol = jnp.transpose(gr)
    bcols = jnp.dot(causal.astype(F32), gcol, preferred_element_type=F32, precision=HI)

    for h in range(M_HEADS):
        lo, hi_ = h * M_HEAD_DIM, (h + 1) * M_HEAD_DIM
        ch = cact[:, lo:hi_]
        chb = ch.astype(BF16)
        q = _dot(chb, wq_ref[h])
        k = _dot(chb, wk_ref[h]) * (M_HEAD_DIM ** -0.5)
        v = uvz_ref[:, M_WIDTH + lo:M_WIDTH + hi_]
        z = uvz_ref[:, 2 * M_WIDTH + lo:2 * M_WIDTH + hi_]
        qb, kb = q.astype(BF16), k.astype(BF16)
        vaug = jnp.concatenate([v, jnp.ones((L, M_HEAD_DIM), F32)], axis=1).astype(BF16)

        b_r = b_rows[M_HEADS + h:M_HEADS + h + 1, :]
        b_c = bcols[:, M_HEADS + h:M_HEADS + h + 1]
        li_r = gr[h:h + 1, :]
        li_c = gcol[:, h:h + 1]
        b_last = b_r[:, L - 1:L]
        m_prev = mst_ref[h][0:1, 0:1]
        caug = cst_ref[h]

        log_d = b_c - b_r + li_r
        inter = b_c + m_prev
        m_t = jnp.maximum(inter, jnp.max(jnp.where(causal, log_d, -jnp.inf), axis=1, keepdims=True))
        d_mat = jnp.where(causal, jnp.exp(log_d - m_t), 0.0)
        inter_w = jnp.exp(inter - m_t)
        s = _dot_nt(qb, kb) * d_mat
        nd = _dot(s.astype(BF16), vaug) + inter_w * _dot(qb, caug.astype(BF16))
        num = nd[:, :M_HEAD_DIM]
        den = nd[:, M_HEAD_DIM:]
        hval = num / jnp.maximum(jnp.abs(den), jnp.exp(-m_t))

        g_c = b_last - b_c + li_c
        m_loc = jnp.max(g_c, axis=0, keepdims=True)
        kw = (jnp.exp(g_c - m_loc) * k).astype(BF16)
        c_new = _dot_tn(kw, vaug)
        m_new = jnp.maximum(b_last + m_prev, m_loc)
        a = jnp.exp(b_last + m_prev - m_new)
        bb = jnp.exp(m_loc - m_new)
        cst_ref[h] = a * caug + bb * c_new
        mst_ref[h] = jnp.broadcast_to(m_new, (SUBLANES, LANES))

        hn = hval * lax.rsqrt(jnp.mean(hval * hval, axis=-1, keepdims=True) + EPS) * ng_ref[:, lo:hi_]
        hn = hn + sk_ref[:, lo:hi_] * ch
        ym_ref[:, lo:hi_] = jax.nn.sigmoid(z) * hn


def _mlstm(uvz, grow, cw, cb, wq, wk, ng, sk, B, S):
    T = B * S
    nc = S // CHUNK
    full = lambda a: pl.BlockSpec(a.shape, lambda b, c: (0,) * a.ndim)
    return pl.pallas_call(
        _mlstm_kernel,
        grid=(B, nc),
        in_specs=[pl.BlockSpec((CHUNK, 3 * M_WIDTH), lambda b, c: (b * nc + c, 0)),
                  pl.BlockSpec((2 * M_HEADS, CHUNK), lambda b, c: (0, b * nc + c)),
                  full(cw), full(cb), full(wq), full(wk), full(ng), full(sk)],
        out_specs=pl.BlockSpec((CHUNK, M_WIDTH), lambda b, c: (b * nc + c, 0)),
        out_shape=jax.ShapeDtypeStruct((T, M_WIDTH), F32),
        scratch_shapes=[pltpu.VMEM((SUBLANES + CHUNK, M_WIDTH), F32),
                        pltpu.VMEM((M_HEADS, M_HEAD_DIM, 2 * M_HEAD_DIM), F32),
                        pltpu.VMEM((M_HEADS, SUBLANES, LANES), F32)],
        compiler_params=_cparams(("arbitrary", "arbitrary")),
        name="mlstm",
    )(uvz, grow, cw, cb, wq, wk, ng, sk)


ATT_BLK = DIL_PATTERNS[0][0] // DIL_PATTERNS[0][1]
ATT_SPAN = DIL_PATTERNS[-1][0]


def _dilattn_kernel(q_ref, kp_ref, kc_ref, vp_ref, vc_ref, bias_ref, o_ref,
                    k2_ref, v2_ref, am_ref, al_ref, ao_ref):
    blk, span = ATT_BLK, ATT_SPAN
    n = pl.program_id(1)
    k2_ref[0:span, :] = kp_ref[...]
    k2_ref[span:, :] = kc_ref[...]
    v2_ref[0:span, :] = vp_ref[...]
    v2_ref[span:, :] = vc_ref[...]
    lane_k = lax.broadcasted_iota(jnp.int32, (2 * blk, LANES), 1)
    first_q = lax.broadcasted_iota(jnp.int32, (blk, LANES), 1) < A_HEAD_DIM

    for p, (window, d) in enumerate(DIL_PATTERNS):
        units = span // blk

        def unit(u, carry, p=p, d=d):
            m_idx, r = u >> (d.bit_length() - 1), u & (d - 1)
            base = m_idx * (blk * d) + r
            if d == 1:
                base = pl.multiple_of(base, blk)
            rows = lambda off: pl.ds(off, blk) if d == 1 else pl.ds(off, blk, stride=d)
            first = jnp.logical_and(n == 0, m_idx == 0).astype(jnp.int32)
            qb = (q_ref[rows(base), :] * (A_HEAD_DIM ** -0.5)).astype(BF16)
            kk = jnp.concatenate([k2_ref[rows(span + base - blk * d), :], k2_ref[rows(span + base), :]], axis=0)
            vv = jnp.concatenate([v2_ref[rows(span + base - blk * d), :], v2_ref[rows(span + base), :]],
                                 axis=0).astype(BF16)
            ms, ls, os_ = [], [], []
            for sub in range(2):
                in_head = (lane_k >= sub * A_HEAD_DIM) & (lane_k < (sub + 1) * A_HEAD_DIM)
                kh = jnp.where(in_head, kk, 0.0).astype(BF16)
                s = _dot_nt(qb, kh) + bias_ref[p, first, sub]
                m = jnp.max(s, axis=1, keepdims=True)
                e = jnp.exp(s - m)
                ms.append(m)
                ls.append(jnp.sum(e, axis=1, keepdims=True))
                os_.append(_dot(e.astype(BF16), vv))
            m_c = jnp.where(first_q, ms[0], ms[1])
            l_c = jnp.where(first_q, ls[0], ls[1])
            o_c = jnp.where(first_q, os_[0], os_[1])
            if p == 0:
                am_ref[rows(base), :] = m_c
                al_ref[rows(base), :] = l_c
                ao_ref[rows(base), :] = o_c
            else:
                m_o = am_ref[rows(base), :]
                m_n = jnp.maximum(m_o, m_c)
                a_o, a_c = jnp.exp(m_o - m_n), jnp.exp(m_c - m_n)
                am_ref[rows(base), :] = m_n
                al_ref[rows(base), :] = a_o * al_ref[rows(base), :] + a_c * l_c
                ao_ref[rows(base), :] = a_o * ao_ref[rows(base), :] + a_c * o_c
            return carry

        lax.fori_loop(0, units, unit, 0, unroll=8)

    o_ref[...] = ao_ref[...] / al_ref[...]


def _dilattn(q, k, v, bias, B, S):
    span = ATT_SPAN
    ns = S // span
    T = B * S
    cur = pl.BlockSpec((span, LANES), lambda b, n, hp: (b * ns + n, hp))
    prev = pl.BlockSpec((span, LANES), lambda b, n, hp: (b * ns + jnp.maximum(n - 1, 0), hp))
    return pl.pallas_call(
        _dilattn_kernel,
        grid=(B, ns, A_HEADS // 2),
        in_specs=[cur, prev, cur, prev, cur,
                  pl.BlockSpec((len(DIL_PATTERNS), 2, 2, ATT_BLK, 2 * ATT_BLK), lambda b, n, hp: (0, 0, hp, 0, 0))],
        out_specs=cur,
        out_shape=jax.ShapeDtypeStruct((T, A_WIDTH), F32),
        scratch_shapes=[pltpu.VMEM((2 * span, LANES), F32), pltpu.VMEM((2 * span, LANES), F32),
                        pltpu.VMEM((span, LANES), F32), pltpu.VMEM((span, LANES), F32),
                        pltpu.VMEM((span, LANES), F32)],
        compiler_params=_cparams(("parallel", "parallel", "parallel")),
        name="dilattn",
    )(q, k, k, v, v, bias)


def _attn_bias_kernel(bkt_ref, rb_ref, out_ref):
    h = pl.program_id(1)
    bkt = bkt_ref[...]
    acc = jnp.full(bkt.shape, -jnp.inf, F32)
    for kk in range(N_BUCKETS):
        acc = jnp.where(bkt == kk, rb_ref[kk, h], acc)
    col = lax.broadcasted_iota(jnp.int32, bkt.shape, 1)
    out_ref[0] = acc
    out_ref[1] = jnp.where(col < ATT_BLK, -jnp.inf, acc)


def _attn_bias(rel_bias):
    blk = ATT_BLK
    i = jnp.arange(blk)[:, None]
    j = jnp.arange(2 * blk)[None, :]
    steps = i + blk - j
    band = (steps >= 0) & (steps <= blk)
    bkt = jnp.stack([jnp.where(band, _t5_bucket(jnp.maximum(steps, 0) * d), -1) for _, d in DIL_PATTERNS])
    P = len(DIL_PATTERNS)
    return pl.pallas_call(
        _attn_bias_kernel,
        grid=(P, A_HEADS),
        in_specs=[pl.BlockSpec((None, blk, 2 * blk), lambda p, h: (p, 0, 0)),
                  pl.BlockSpec(memory_space=pltpu.SMEM)],
        out_specs=pl.BlockSpec((None, 2, None, blk, 2 * blk), lambda p, h: (p, 0, h, 0, 0)),
        out_shape=jax.ShapeDtypeStruct((P, 2, A_HEADS, blk, 2 * blk), F32),
        compiler_params=_cparams(("parallel", "parallel")),
        name="attn_bias",
    )(bkt.astype(jnp.int32), rel_bias)


def _t5_bucket(dist):
    max_exact = N_BUCKETS // 2
    nf = jnp.maximum(dist, 1).astype(F32)
    large = max_exact + (jnp.log(nf / max_exact) / np.log(MAX_DISTANCE / max_exact)
                         * (N_BUCKETS - max_exact)).astype(jnp.int32)
    large = jnp.minimum(large, N_BUCKETS - 1)
    return jnp.where(dist < max_exact, dist, large)


def _outproj_kernel(x_ref, ym_ref, ya_ref, wo_ref, g2_ref, x2_ref, h2t_ref):
    y = (_dot(ym_ref[...].astype(BF16), wo_ref[0:M_WIDTH, :])
         + _dot(ya_ref[...].astype(BF16), wo_ref[M_WIDTH:, :]))
    x2 = x_ref[...] + y
    x2_ref[...] = x2
    h2 = x2 * lax.rsqrt(jnp.mean(x2 * x2, axis=-1, keepdims=True) + EPS) * g2_ref[...]
    h2t_ref[...] = jnp.transpose(h2).astype(BF16)


def _outproj(x2d, ym, ya, wo, g2, tm):
    T, D = x2d.shape
    half = pl.BlockSpec((tm, M_WIDTH), lambda i: (i, 0))
    full = lambda a: pl.BlockSpec(a.shape, lambda i: (0,) * a.ndim)
    return pl.pallas_call(
        _outproj_kernel,
        grid=(T // tm,),
        in_specs=[pl.BlockSpec((tm, D), lambda i: (i, 0)), half, half, full(wo), full(g2)],
        out_specs=[pl.BlockSpec((tm, D), lambda i: (i, 0)), pl.BlockSpec((D, tm), lambda i: (0, i))],
        out_shape=[jax.ShapeDtypeStruct((T, D), F32), jax.ShapeDtypeStruct((D, T), BF16)],
        compiler_params=_cparams(("parallel",)),
        name="outproj",
    )(x2d, ym, ya, wo, g2)


def _peer_keys_kernel(keys_ref, wqt_ref, out_ref):
    out_ref[...] = jnp.dot(keys_ref[...], wqt_ref[...], preferred_element_type=F32,
                           precision=HI).astype(out_ref.dtype)


def _peer_keys(keys, wqt):
    _, H, K, C = keys.shape
    D = wqt.shape[-1]
    return pl.pallas_call(
        _peer_keys_kernel,
        grid=(2, H),
        in_specs=[pl.BlockSpec((None, None, K, C), lambda a, h: (a, h, 0, 0)),
                  pl.BlockSpec((None, None, C, D), lambda a, h: (a, h, 0, 0))],
        out_specs=pl.BlockSpec((None, None, K, D), lambda a, h: (a, h, 0, 0)),
        out_shape=jax.ShapeDtypeStruct((2, H, K, D), BF16),
        compiler_params=_cparams(("parallel", "parallel")),
        name="peer_keys",
    )(keys, wqt)


def _oddeven_merge_sort_pairs(n):
    pairs = []

    def merge(lo, hi, r):
        step = r * 2
        if step < hi - lo:
            merge(lo, hi, step)
            merge(lo + r, hi, step)
            for i in range(lo + r, hi - r, step):
                pairs.append((i, i + r))
        else:
            pairs.append((lo, lo + r))

    def sort(lo, hi):
        if hi - lo >= 1:
            mid = lo + (hi - lo) // 2
            sort(lo, mid)
            sort(mid + 1, hi)
            merge(lo, hi, 1)

    sort(0, n - 1)
    return pairs


_SORT16 = _oddeven_merge_sort_pairs(PEER_TOPK)


def _sort_desc(w):
    w = list(w)
    for a, b in _SORT16:
        hi, lo = jnp.maximum(w[a], w[b]), jnp.minimum(w[a], w[b])
        w[a], w[b] = hi, lo
    return w


def _merge_top(a, b):
    n = len(a)
    w = [jnp.maximum(a[i], b[n - 1 - i]) for i in range(n)]
    half = n // 2
    while half >= 1:
        for start in range(0, n, 2 * half):
            for i in range(start, start + half):
                hi, lo = jnp.maximum(w[i], w[i + half]), jnp.minimum(w[i], w[i + half])
                w[i], w[i + half] = hi, lo
        half //= 2
    return w


def _peer_topk_kernel(wk_ref, h2t_ref, ub_ref, nb_ref, vb_ref, rk_ref, sc_ref):
    H, K = PEER_HEADS, N_KEYS
    tb = h2t_ref.shape[1]
    h2t = h2t_ref[...]
    for a in range(2):
        for h in range(H):
            sc_ref[a, h] = _dot(wk_ref[a, h], h2t)

    sub = lax.broadcasted_iota(jnp.int32, (SUBLANES, LANES), 0)
    zeros = jnp.zeros((SUBLANES, LANES), F32)
    for lt in range(tb // LANES):
        ls = slice(lt * LANES, (lt + 1) * LANES)
        packed = []
        for a in range(2):
            acc = [None] * PEER_TOPK
            for h in range(H):
                w = [sc_ref[a, h, v * SUBLANES:(v + 1) * SUBLANES, ls] for v in range(K // SUBLANES)]
                w = _sort_desc(w)
                for shift in (4, 2, 1):
                    w = _merge_top(w, [pltpu.roll(x, shift, 0) for x in w])
                for i in range(PEER_TOPK):
                    acc[i] = w[i] if h == 0 else jnp.where(sub == h, w[i], acc[i])
            packed.append(acc)
        v1, v2 = packed
        rows = [[v1[a] + v2[b] for b in range(PEER_TOPK // (a + 1))] for a in range(PEER_TOPK)]
        neg = jnp.full((SUBLANES, LANES), -jnp.inf, F32)
        l0 = rows[0]
        l1 = _sort_desc(rows[1] + rows[2] + rows[4])
        l2 = _sort_desc(rows[3] + rows[5] + rows[6] + rows[7] + [rows[a][0] for a in range(8, 14)])
        l3 = [rows[14][0], rows[15][0]]
        l3 = [jnp.maximum(l3[0], l3[1]), jnp.minimum(l3[0], l3[1])] + [neg] * (PEER_TOPK - 2)
        vc = _merge_top(_merge_top(l0, l1), _merge_top(l2, l3))
        top, tau = vc[0], vc[PEER_TOPK - 1]
        zsum = zeros
        for i in range(PEER_TOPK):
            zsum = zsum + jnp.exp(vc[i] - top)
        inv_z = 1.0 / zsum
        n_sel = []
        for a in range(PEER_TOPK):
            cnt = zeros
            for cand in rows[a]:
                cnt = cnt + jnp.where(cand >= tau, 1.0, 0.0)
            n_sel.append(cnt)
        m1, m2 = v1[0], v2[0]
        for h in range(H):
            bc = lambda x: jnp.broadcast_to(x[h:h + 1, :], (SUBLANES, LANES))
            w1 = [bc(x) for x in v1]
            w2 = [bc(x) for x in v2]
            na = [bc(x) for x in n_sel]
            m1h, m2h, izh = bc(m1), bc(m2), bc(inv_z)
            ub, nb, vb, rk = [], [], [], []
            for v in range(K // SUBLANES):
                s1 = sc_ref[0, h, v * SUBLANES:(v + 1) * SUBLANES, ls]
                s2 = sc_ref[1, h, v * SUBLANES:(v + 1) * SUBLANES, ls]
                nbv, rkv = zeros, zeros
                for a in range(PEER_TOPK):
                    nbv = jnp.where(s1 == w1[a], na[a], nbv)
                    rkv = jnp.where(w2[a] > s2, float(a + 1), rkv)
                ub.append(jnp.exp(s1 - m1h) * izh)
                vb.append(jnp.exp(s2 - m2h))
                nb.append(nbv)
                rk.append(rkv)
            ub_ref[h, :, ls] = jnp.concatenate(ub, axis=0)
            nb_ref[h, :, ls] = jnp.concatenate(nb, axis=0)
            vb_ref[h, :, ls] = jnp.concatenate(vb, axis=0).astype(BF16)
            rk_ref[h, :, ls] = jnp.concatenate(rk, axis=0).astype(BF16)


def _peer_topk(wk, h2t, tb):
    D, T = h2t.shape
    H, K = PEER_HEADS, N_KEYS
    spec = pl.BlockSpec((H, K, tb), lambda i: (0, 0, i))
    return pl.pallas_call(
        _peer_topk_kernel,
        grid=(T // tb,),
        in_specs=[pl.BlockSpec(wk.shape, lambda i: (0, 0, 0, 0)), pl.BlockSpec((D, tb), lambda i: (0, i))],
        out_specs=[spec] * 4,
        out_shape=[jax.ShapeDtypeStruct((H, K, T), F32)] * 2 + [jax.ShapeDtypeStruct((H, K, T), BF16)] * 2,
        scratch_shapes=[pltpu.VMEM((2, H, K, tb), F32)],
        compiler_params=_cparams(("parallel",)),
        name="peer_topk",
    )(wk, h2t)


def _peer_main_kernel(h2t_ref, ue_ref, vet_ref, ub_ref, nb_ref, vb_ref, rk_ref, x2_ref, out_ref,
                      acc_ref, a_ref):
    c = pl.program_id(1)
    n_i = ue_ref.shape[0] // N_KEYS

    @pl.when(c == 0)
    def _():
        acc_ref[...] = jnp.zeros_like(acc_ref)

    h2t = h2t_ref[...]
    for ii in range(n_i):
        rs = slice(ii * N_KEYS, (ii + 1) * N_KEYS)
        pre = _dot(ue_ref[rs, :], h2t)
        act = (pre * (0.5 * lax.erf(pre * (2.0 ** -0.5)) + 0.5)).astype(BF16)
        gate = jnp.zeros(pre.shape, BF16)
        for h in range(PEER_HEADS):
            row = lambda ref: jnp.broadcast_to(ref[h, ii:ii + 1, :], pre.shape).astype(BF16)
            gate = gate + jnp.where(rk_ref[h] < row(nb_ref), row(ub_ref) * vb_ref[h], 0)
        a_ref[rs, :] = act * gate
    acc_ref[...] += _dot(vet_ref[...], a_ref[...])

    @pl.when(c == pl.num_programs(1) - 1)
    def _():
        out_ref[...] = x2_ref[...] + jnp.transpose(acc_ref[...])


def _peer_main(h2t, ue, vet, ub, nb, vb, rk, x2, tb, ec):
    D, T = h2t.shape
    E = ue.shape[0]
    H, K = PEER_HEADS, N_KEYS
    row = pl.BlockSpec((H, ec // K, tb), lambda t, c: (0, c, t))
    col = pl.BlockSpec((H, K, tb), lambda t, c: (0, 0, t))
    return pl.pallas_call(
        _peer_main_kernel,
        grid=(T // tb, E // ec),
        in_specs=[pl.BlockSpec((D, tb), lambda t, c: (0, t)),
                  pl.BlockSpec((ec, D), lambda t, c: (c, 0)),
                  pl.BlockSpec((D, ec), lambda t, c: (0, c)),
                  row, row, col, col,
                  pl.BlockSpec((tb, D), lambda t, c: (t, 0))],
        out_specs=pl.BlockSpec((tb, D), lambda t, c: (t, 0)),
        out_shape=jax.ShapeDtypeStruct((T, D), F32),
        scratch_shapes=[pltpu.VMEM((D, tb), F32), pltpu.VMEM((ec, tb), BF16)],
        compiler_params=_cparams(("parallel", "arbitrary")),
        name="peer_main",
    )(h2t, ue, vet, ub, nb, vb, rk, x2)


def _block_rows(T, want):
    tm = min(T, want)
    assert T % tm == 0
    return tm


def _layer(x2d, B, S, norm1_g, w_in, conv_w, conv_b, wq_m, wk_m, ig_b, fg_b, mh_norm_g, skip_m,
           qn_g, kn_g, rel_bias, w_out, norm2_g, w_query, sub_keys1, sub_keys2, expert_u, expert_v):
    T, D = x2d.shape
    o_vm, o_z, o_i, o_f, o_q = M_WIDTH, 2 * M_WIDTH, 3 * M_WIDTH, 3 * M_WIDTH + M_HEADS, 3 * M_WIDTH + 2 * M_HEADS
    wm = w_in[:, :o_i].astype(BF16)
    wa = w_in[:, o_q:].astype(BF16)
    wgt = jnp.transpose(w_in[:, o_i:o_q]).astype(BF16)
    gb = jnp.concatenate([ig_b, fg_b])[:, None]
    seg = np.arange(A_WIDTH) // A_HEAD_DIM
    bd = jnp.asarray(seg[:, None] == seg[None, :], BF16)
    qkg = jnp.stack([qn_g.reshape(-1), kn_g.reshape(-1)])
    tm = _block_rows(T, 512)
    uvz, qn, kn, va, grow = _inproj(x2d, norm1_g[None, :], wm, wa, wgt, gb, bd, qkg, tm)

    ym = _mlstm(uvz, grow, conv_w, conv_b[None, :], wq_m.astype(BF16), wk_m.astype(BF16),
                mh_norm_g.reshape(1, -1), skip_m.reshape(1, -1), B, S)

    ya = _dilattn(qn, kn, va, _attn_bias(rel_bias), B, S)

    x2, h2t = _outproj(x2d, ym, ya, w_out.astype(BF16), norm2_g[None, :], tm)

    keys = jnp.stack([sub_keys1, sub_keys2])
    wqt = jnp.transpose(w_query.reshape(D, PEER_HEADS, 2, PEER_QDIM // 2), (2, 1, 3, 0))
    wk = _peer_keys(keys, wqt)
    ub, nb, vb, rk = _peer_topk(wk, h2t, _block_rows(T, 256))
    ue = expert_u.astype(BF16)
    vet = jnp.transpose(expert_v).astype(BF16)
    return _peer_main(h2t, ue, vet, ub, nb, vb, rk, x2, _block_rows(T, 512), 2048)


def kernel(x, norm1_g, w_in, conv_w, conv_b, wq_m, wk_m, ig_b, fg_b, mh_norm_g, skip_m, qn_g, kn_g,
           rel_bias, w_out, norm2_g, w_query, sub_keys1, sub_keys2, expert_u, expert_v):
    B, S, D = x.shape
    assert S % DIL_PATTERNS[-1][0] == 0 and S % CHUNK == 0
    x2d = x.reshape(B * S, D)
    for l in range(norm1_g.shape[0]):
        x2d = _layer(x2d, B, S, norm1_g[l], w_in[l], conv_w[l], conv_b[l], wq_m[l], wk_m[l], ig_b[l],
                     fg_b[l], mh_norm_g[l], skip_m[l], qn_g[l], kn_g[l], rel_bias, w_out[l], norm2_g[l],
                     w_query[l], sub_keys1[l], sub_keys2[l], expert_u[l], expert_v[l])
    return x2d.reshape(B, S, D)
```

```python
import functools
import math

import numpy as np
import jax
import jax.numpy as jnp
from jax import lax
from jax.experimental import pallas as pl
from jax.experimental.pallas import tpu as pltpu

EPS = 1e-6
M_HEADS = 4
M_HEAD_DIM = 128
M_WIDTH = M_HEADS * M_HEAD_DIM
CONV_W = 4
CHUNK = 128
A_HEADS = 8
A_HEAD_DIM = 64
A_WIDTH = A_HEADS * A_HEAD_DIM
DIL_PATTERNS = ((128, 1), (512, 4), (2048, 16))
N_BUCKETS = 32
MAX_DISTANCE = 2048
PEER_HEADS = 8
N_KEYS = 128
PEER_QDIM = 256
PEER_TOPK = 16

LANES = 128
SUBLANES = 8
VMEM_LIMIT = 56 * 1024 * 1024

BF16 = jnp.bfloat16
F32 = jnp.float32
HI = lax.Precision.HIGHEST


def _cparams(sem):
    return pltpu.CompilerParams(dimension_semantics=sem, vmem_limit_bytes=VMEM_LIMIT)


def _dot(a, b):
    return jnp.dot(a, b, preferred_element_type=F32)


def _dot_nt(a, b):
    return lax.dot_general(a, b, (((1,), (1,)), ((), ())), preferred_element_type=F32)


def _dot_tn(a, b):
    return lax.dot_general(a, b, (((0,), (0,)), ((), ())), preferred_element_type=F32)


def _inproj_kernel(x_ref, g1_ref, wm_ref, wa_ref, wgt_ref, gb_ref, bd_ref, qkg_ref,
                   uvz_ref, q_ref, k_ref, v_ref, grow_ref):
    x = x_ref[...]
    h = x * lax.rsqrt(jnp.mean(x * x, axis=-1, keepdims=True) + EPS) * g1_ref[...]
    hb = h.astype(BF16)
    uvz_ref[...] = _dot(hb, wm_ref[...])
    qkv = _dot(hb, wa_ref[...])
    bd = bd_ref[...]

    def head_norm(t, g):
        sq = t * t
        hi = sq.astype(BF16)
        lo = (sq - hi.astype(F32)).astype(BF16)
        ms = (_dot(hi, bd) + _dot(lo, bd)) * (1.0 / A_HEAD_DIM)
        return t * lax.rsqrt(ms + EPS) * g

    q_ref[...] = head_norm(qkv[:, :A_WIDTH], qkg_ref[0:1, :])
    k_ref[...] = head_norm(qkv[:, A_WIDTH:2 * A_WIDTH], qkg_ref[1:2, :])
    v_ref[...] = qkv[:, 2 * A_WIDTH:]
    gr = _dot_nt(wgt_ref[...], hb) + gb_ref[...]
    lf = jnp.minimum(gr, 0.0) - jnp.log1p(jnp.exp(-jnp.abs(gr)))
    row = lax.broadcasted_iota(jnp.int32, gr.shape, 0)
    grow_ref[...] = jnp.where(row < M_HEADS, gr, lf)


def _inproj(x2d, g1, wm, wa, wgt, gb, bd, qkg, tm):
    T, D = x2d.shape
    full = lambda a: pl.BlockSpec(a.shape, lambda i: (0,) * a.ndim)
    return pl.pallas_call(
        _inproj_kernel,
        grid=(T // tm,),
        in_specs=[pl.BlockSpec((tm, D), lambda i: (i, 0)), full(g1), full(wm), full(wa), full(wgt),
                  full(gb), full(bd), full(qkg)],
        out_specs=[pl.BlockSpec((tm, 3 * M_WIDTH), lambda i: (i, 0)),
                   pl.BlockSpec((tm, A_WIDTH), lambda i: (i, 0)),
                   pl.BlockSpec((tm, A_WIDTH), lambda i: (i, 0)),
                   pl.BlockSpec((tm, A_WIDTH), lambda i: (i, 0)),
                   pl.BlockSpec((2 * M_HEADS, tm), lambda i: (0, i))],
        out_shape=[jax.ShapeDtypeStruct((T, 3 * M_WIDTH), F32),
                   jax.ShapeDtypeStruct((T, A_WIDTH), F32),
                   jax.ShapeDtypeStruct((T, A_WIDTH), F32),
                   jax.ShapeDtypeStruct((T, A_WIDTH), F32),
                   jax.ShapeDtypeStruct((2 * M_HEADS, T), F32)],
        compiler_params=_cparams(("parallel",)),
        name="inproj",
    )(x2d, g1, wm, wa, wgt, gb, bd, qkg)


def _mlstm_kernel(*refs):
    nb = len(refs) - 11
    uvz_all, grow_refs = refs[0], refs[1:1 + nb]
    cw_ref, cb_ref, wq_ref, wk_ref, ng_ref, sk_ref, ym_all, ubuf_all, cst_ref, mst_ref = refs[1 + nb:]
    L = CHUNK

    @pl.when(pl.program_id(0) == 0)
    def _():
        ubuf_all[:, 0:SUBLANES, :] = jnp.zeros((nb, SUBLANES, M_WIDTH), F32)
        cst_ref[...] = jnp.zeros_like(cst_ref)
        mst_ref[...] = jnp.zeros_like(mst_ref)

    rr = lax.broadcasted_iota(jnp.int32, (L, L), 0)
    cc = lax.broadcasted_iota(jnp.int32, (L, L), 1)
    causal = cc <= rr
    tri_u = (rr <= cc).astype(F32)
    for b in range(nb):
        _mlstm_chunk(uvz_all.at[b], grow_refs[b], cw_ref, cb_ref, wq_ref, wk_ref, ng_ref, sk_ref,
                     ym_all.at[b], ubuf_all.at[b], cst_ref, mst_ref, b * M_HEADS, causal, tri_u)


def _mlstm_chunk(uvz_ref, grow_ref, cw_ref, cb_ref, wq_ref, wk_ref, ng_ref, sk_ref,
                 ym_ref, ubuf_ref, cst_ref, mst_ref, st0, causal, tri_u):
    L = CHUNK
    u = uvz_ref[:, 0:M_WIDTH]
    ubuf_ref[SUBLANES:SUBLANES + L, :] = u
    conv = cb_ref[...] + jnp.zeros((L, M_WIDTH), F32)
    for w in range(CONV_W):
        conv = conv + ubuf_ref[pl.ds(SUBLANES - (CONV_W - 1) + w, L), :] * cw_ref[w:w + 1, :]
    ubuf_ref[0:SUBLANES, :] = u[L - SUBLANES:, :]
    cact = conv * jax.nn.sigmoid(conv)

    gr = grow_ref[...]
    b_rows = jnp.dot(gr, tri_u, preferred_element_type=F32, precision=HI)
    gcol = jnp.transpose(gr)
    bcols = jnp.dot(causal.astype(F32), gcol, preferred_element_type=F32, precision=HI)

    for h in range(M_HEADS):
        lo, hi_ = h * M_HEAD_DIM, (h + 1) * M_HEAD_DIM
        ch = cact[:, lo:hi_]
        chb = ch.astype(BF16)
        q = _dot(chb, wq_ref[h])
        k = _dot(chb, wk_ref[h]) * (M_HEAD_DIM ** -0.5)
        v = uvz_ref[:, M_WIDTH + lo:M_WIDTH + hi_]
        z = uvz_ref[:, 2 * M_WIDTH + lo:2 * M_WIDTH + hi_]
        qb, kb = q.astype(BF16), k.astype(BF16)
        vaug = jnp.concatenate([v, jnp.ones((L, M_HEAD_DIM), F32)], axis=1).astype(BF16)

        b_r = b_rows[M_HEADS + h:M_HEADS + h + 1, :]
        b_c = bcols[:, M_HEADS + h:M_HEADS + h + 1]
        li_r = gr[h:h + 1, :]
        li_c = gcol[:, h:h + 1]
        b_last = b_r[:, L - 1:L]
        m_prev = mst_ref[st0 + h][0:1, 0:1]
        caug = cst_ref[st0 + h]

        log_d = b_c - b_r + li_r
        inter = b_c + m_prev
        m_t = jnp.maximum(inter, jnp.max(jnp.where(causal, log_d, -jnp.inf), axis=1, keepdims=True))
        d_mat = jnp.where(causal, jnp.exp(log_d - m_t), 0.0)
        inter_w = jnp.exp(inter - m_t)
        s = _dot_nt(qb, kb) * d_mat
        nd = _dot(s.astype(BF16), vaug) + inter_w * _dot(qb, caug.astype(BF16))
        num = nd[:, :M_HEAD_DIM]
        den = nd[:, M_HEAD_DIM:]
        hval = num / jnp.maximum(jnp.abs(den), jnp.exp(-m_t))

        g_c = b_last - b_c + li_c
        m_loc = jnp.max(g_c, axis=0, keepdims=True)
        kw = (jnp.exp(g_c - m_loc) * k).astype(BF16)
        c_new = _dot_tn(kw, vaug)
        m_new = jnp.maximum(b_last + m_prev, m_loc)
        a = jnp.exp(b_last + m_prev - m_new)
        bb = jnp.exp(m_loc - m_new)
        cst_ref[st0 + h] = a * caug + bb * c_new
        mst_ref[st0 + h] = jnp.broadcast_to(m_new, (SUBLANES, LANES))

        hn = hval * lax.rsqrt(jnp.mean(hval * hval, axis=-1, keepdims=True) + EPS) * ng_ref[:, lo:hi_]
        hn = hn + sk_ref[:, lo:hi_] * ch
        ym_ref[:, lo:hi_] = jax.nn.sigmoid(z) * hn


def _mlstm(uvz, grow, cw, cb, wq, wk, ng, sk, B, S):
    nc = S // CHUNK
    full = lambda a: pl.BlockSpec(a.shape, lambda c: (0,) * a.ndim)
    gate_specs = [pl.BlockSpec((2 * M_HEADS, CHUNK), lambda c, b=b: (0, b * nc + c)) for b in range(B)]
    ym = pl.pallas_call(
        _mlstm_kernel,
        grid=(nc,),
        in_specs=[pl.BlockSpec((B, CHUNK, 3 * M_WIDTH), lambda c: (0, c, 0))] + gate_specs
                 + [full(cw), full(cb), full(wq), full(wk), full(ng), full(sk)],
        out_specs=pl.BlockSpec((B, CHUNK, M_WIDTH), lambda c: (0, c, 0)),
        out_shape=jax.ShapeDtypeStruct((B, S, M_WIDTH), F32),
        scratch_shapes=[pltpu.VMEM((B, SUBLANES + CHUNK, M_WIDTH), F32),
                        pltpu.VMEM((B * M_HEADS, M_HEAD_DIM, 2 * M_HEAD_DIM), F32),
                        pltpu.VMEM((B * M_HEADS, SUBLANES, LANES), F32)],
        compiler_params=_cparams(("arbitrary",)),
        name="mlstm",
    )(uvz.reshape(B, S, 3 * M_WIDTH), *([grow] * B), cw, cb, wq, wk, ng, sk)
    return ym.reshape(B * S, M_WIDTH)


ATT_BLK = DIL_PATTERNS[0][0] // DIL_PATTERNS[0][1]
ATT_SPAN = DIL_PATTERNS[-1][0]


ATT_R = DIL_PATTERNS[1][1]
assert [d for _, d in DIL_PATTERNS] == [1, ATT_R, ATT_R * ATT_R]
ATT_UNROLL = 8


def _dilattn_kernel(q_ref, kp_ref, kc_ref, vp_ref, vc_ref, bias_ref, o_ref,
                    q4_ref, k4_ref, v4_ref, k1_ref, v1_ref, a1_ref, a4_ref):
    blk, span, R = ATT_BLK, ATT_SPAN, ATT_R
    sub = span // R
    n = pl.program_id(1)
    scale = A_HEAD_DIM ** -0.5
    for r in range(R):
        res = pl.ds(r, sub, stride=R)
        q4_ref[r] = q_ref[res, :] * scale
        k4_ref[r, 0:sub] = kp_ref[res, :]
        k4_ref[r, sub:] = kc_ref[res, :]
        v4_ref[r, 0:sub] = vp_ref[res, :]
        v4_ref[r, sub:] = vc_ref[res, :]
    k1_ref[0:blk] = kp_ref[span - blk:, :]
    k1_ref[blk:] = kc_ref[...]
    v1_ref[0:blk] = vp_ref[span - blk:, :]
    v1_ref[blk:] = vc_ref[...]
    first_q = lax.broadcasted_iota(jnp.int32, (blk, LANES), 1) < A_HEAD_DIM

    def attend(q, kk, vv, bias_sel):
        kb, vb = kk.astype(BF16), vv.astype(BF16)
        ms, ls, os_ = [], [], []
        for s_ in range(2):
            qh = jnp.where(first_q, q, 0.0) if s_ == 0 else jnp.where(first_q, 0.0, q)
            s = _dot_nt(qh.astype(BF16), kb) + bias_sel(s_)
            m = jnp.max(s, axis=1, keepdims=True)
            e = jnp.exp(s - m)
            ms.append(m)
            ls.append(jnp.sum(e, axis=1, keepdims=True))
            os_.append(_dot(e.astype(BF16), vb))
        return (jnp.where(first_q, ms[0], ms[1]), jnp.where(first_q, ls[0], ls[1]),
                jnp.where(first_q, os_[0], os_[1]))

    def unit_d1(m, carry):
        base = pl.multiple_of(m * blk, blk)
        first = jnp.logical_and(n == 0, m == 0).astype(jnp.int32)
        m_c, l_c, o_c = attend(q_ref[pl.ds(base, blk), :] * scale, k1_ref[pl.ds(base, 2 * blk), :],
                               v1_ref[pl.ds(base, 2 * blk), :], lambda s_: bias_ref[0, first, s_])
        a1_ref[0, pl.ds(base, blk), :] = m_c
        a1_ref[1, pl.ds(base, blk), :] = l_c
        a1_ref[2, pl.ds(base, blk), :] = o_c
        return carry

    def unit_d4(u, carry):
        m4, r = u >> (R.bit_length() - 1), u & (R - 1)
        off = pl.multiple_of(m4 * blk, blk)
        first = jnp.logical_and(n == 0, m4 == 0).astype(jnp.int32)
        keys = pl.ds(sub - blk + off, 2 * blk)
        m_c, l_c, o_c = attend(q4_ref[r, pl.ds(off, blk), :], k4_ref[r, keys, :], v4_ref[r, keys, :],
                               lambda s_: bias_ref[1, first, s_])
        a4_ref[0, r, pl.ds(off, blk), :] = m_c
        a4_ref[1, r, pl.ds(off, blk), :] = l_c
        a4_ref[2, r, pl.ds(off, blk), :] = o_c
        return carry

    def unit_d16(u, carry):
        r, rp = u & (R - 1), u >> (R.bit_length() - 1)
        first = (n == 0).astype(jnp.int32)
        own, cur = pl.ds(rp, blk, stride=R), pl.ds(sub + rp, blk, stride=R)
        kk = jnp.concatenate([k4_ref[r, own, :], k4_ref[r, cur, :]], axis=0)
        vv = jnp.concatenate([v4_ref[r, own, :], v4_ref[r, cur, :]], axis=0)
        m_c, l_c, o_c = attend(q4_ref[r, own, :], kk, vv, lambda s_: bias_ref[2, first, s_])
        m_o = a4_ref[0, r, own, :]
        m_n = jnp.maximum(m_o, m_c)
        a_o, a_c = jnp.exp(m_o - m_n), jnp.exp(m_c - m_n)
        a4_ref[0, r, own, :] = m_n
        a4_ref[1, r, own, :] = a_o * a4_ref[1, r, own, :] + a_c * l_c
        a4_ref[2, r, own, :] = a_o * a4_ref[2, r, own, :] + a_c * o_c
        return carry

    units = span // blk
    lax.fori_loop(0, units, unit_d1, 0, unroll=ATT_UNROLL)
    lax.fori_loop(0, units, unit_d4, 0, unroll=ATT_UNROLL)
    lax.fori_loop(0, units, unit_d16, 0, unroll=ATT_UNROLL)

    for r in range(R):
        res = pl.ds(r, sub, stride=R)
        m1, m4 = a1_ref[0, res, :], a4_ref[0, r]
        m_n = jnp.maximum(m1, m4)
        e1, e4 = jnp.exp(m1 - m_n), jnp.exp(m4 - m_n)
        o_ref[res, :] = ((e1 * a1_ref[2, res, :] + e4 * a4_ref[2, r])
                         / (e1 * a1_ref[1, res, :] + e4 * a4_ref[1, r]))


def _dilattn(q, k, v, bias, B, S):
    span, blk, R = ATT_SPAN, ATT_BLK, ATT_R
    ns = S // span
    T = B * S
    cur = pl.BlockSpec((span, LANES), lambda b, n, hp: (b * ns + n, hp))
    prev = pl.BlockSpec((span, LANES), lambda b, n, hp: (b * ns + jnp.maximum(n - 1, 0), hp))
    return pl.pallas_call(
        _dilattn_kernel,
        grid=(B, ns, A_HEADS // 2),
        in_specs=[cur, prev, cur, prev, cur,
                  pl.BlockSpec((len(DIL_PATTERNS), 2, 2, blk, 2 * blk), lambda b, n, hp: (0, 0, hp, 0, 0))],
        out_specs=cur,
        out_shape=jax.ShapeDtypeStruct((T, A_WIDTH), F32),
        scratch_shapes=[pltpu.VMEM((R, span // R, LANES), F32),
                        pltpu.VMEM((R, 2 * span // R, LANES), F32),
                        pltpu.VMEM((R, 2 * span // R, LANES), F32),
                        pltpu.VMEM((blk + span, LANES), F32),
                        pltpu.VMEM((blk + span, LANES), F32),
                        pltpu.VMEM((3, span, LANES), F32),
                        pltpu.VMEM((3, R, span // R, LANES), F32)],
        compiler_params=_cparams(("parallel", "parallel", "parallel")),
        name="dilattn",
    )(q, k, k, v, v, bias)


def _attn_bias_kernel(bkt_ref, rb_ref, out_ref):
    h = pl.program_id(1)
    bkt = bkt_ref[...]
    acc = jnp.full(bkt.shape, -jnp.inf, F32)
    for kk in range(N_BUCKETS):
        acc = jnp.where(bkt == kk, rb_ref[kk, h], acc)
    col = lax.broadcasted_iota(jnp.int32, bkt.shape, 1)
    out_ref[0] = acc
    out_ref[1] = jnp.where(col < ATT_BLK, -jnp.inf, acc)


def _attn_bias(rel_bias):
    blk = ATT_BLK
    i = jnp.arange(blk)[:, None]
    j = jnp.arange(2 * blk)[None, :]
    steps = i + blk - j
    band = (steps >= 0) & (steps <= blk)
    bkt = jnp.stack([jnp.where(band, _t5_bucket(jnp.maximum(steps, 0) * d), -1) for _, d in DIL_PATTERNS])
    P = len(DIL_PATTERNS)
    return pl.pallas_call(
        _attn_bias_kernel,
        grid=(P, A_HEADS),
        in_specs=[pl.BlockSpec((None, blk, 2 * blk), lambda p, h: (p, 0, 0)),
                  pl.BlockSpec(memory_space=pltpu.SMEM)],
        out_specs=pl.BlockSpec((None, 2, None, blk, 2 * blk), lambda p, h: (p, 0, h, 0, 0)),
        out_shape=jax.ShapeDtypeStruct((P, 2, A_HEADS, blk, 2 * blk), F32),
        compiler_params=_cparams(("parallel", "parallel")),
        name="attn_bias",
    )(bkt.astype(jnp.int32), rel_bias)


def _t5_bucket(dist):
    max_exact = N_BUCKETS // 2
    nf = jnp.maximum(dist, 1).astype(F32)
    large = max_exact + (jnp.log(nf / max_exact) / np.log(MAX_DISTANCE / max_exact)
                         * (N_BUCKETS - max_exact)).astype(jnp.int32)
    large = jnp.minimum(large, N_BUCKETS - 1)
    return jnp.where(dist < max_exact, dist, large)


def _outproj_kernel(x_ref, ym_ref, ya_ref, wo_ref, g2_ref, x2_ref, h2t_ref):
    y = (_dot(ym_ref[...].astype(BF16), wo_ref[0:M_WIDTH, :])
         + _dot(ya_ref[...].astype(BF16), wo_ref[M_WIDTH:, :]))
    x2 = x_ref[...] + y
    x2_ref[...] = x2
    h2 = x2 * lax.rsqrt(jnp.mean(x2 * x2, axis=-1, keepdims=True) + EPS) * g2_ref[...]
    h2t_ref[...] = jnp.transpose(h2).astype(BF16)


def _outproj(x2d, ym, ya, wo, g2, tm):
    T, D = x2d.shape
    half = pl.BlockSpec((tm, M_WIDTH), lambda i: (i, 0))
    full = lambda a: pl.BlockSpec(a.shape, lambda i: (0,) * a.ndim)
    return pl.pallas_call(
        _outproj_kernel,
        grid=(T // tm,),
        in_specs=[pl.BlockSpec((tm, D), lambda i: (i, 0)), half, half, full(wo), full(g2)],
        out_specs=[pl.BlockSpec((tm, D), lambda i: (i, 0)), pl.BlockSpec((D, tm), lambda i: (0, i))],
        out_shape=[jax.ShapeDtypeStruct((T, D), F32), jax.ShapeDtypeStruct((D, T), BF16)],
        compiler_params=_cparams(("parallel",)),
        name="outproj",
    )(x2d, ym, ya, wo, g2)


def _peer_keys_kernel(keys_ref, wqt_ref, out_ref):
    out_ref[...] = jnp.dot(keys_ref[...], wqt_ref[...], preferred_element_type=F32,
                           precision=HI).astype(out_ref.dtype)


def _peer_keys(keys, wqt):
    _, H, K, C = keys.shape
    D = wqt.shape[-1]
    return pl.pallas_call(
        _peer_keys_kernel,
        grid=(2, H),
        in_specs=[pl.BlockSpec((None, None, K, C), lambda a, h: (a, h, 0, 0)),
                  pl.BlockSpec((None, None, C, D), lambda a, h: (a, h, 0, 0))],
        out_specs=pl.BlockSpec((None, None, K, D), lambda a, h: (a, h, 0, 0)),
        out_shape=jax.ShapeDtypeStruct((2, H, K, D), BF16),
        compiler_params=_cparams(("parallel", "parallel")),
        name="peer_keys",
    )(keys, wqt)


def _oddeven_merge_sort_pairs(n):
    pairs = []

    def merge(lo, hi, r):
        step = r * 2
        if step < hi - lo:
            merge(lo, hi, step)
            merge(lo + r, hi, step)
            for i in range(lo + r, hi - r, step):
                pairs.append((i, i + r))
        else:
            pairs.append((lo, lo + r))

    def sort(lo, hi):
        if hi - lo >= 1:
            mid = lo + (hi - lo) // 2
            sort(lo, mid)
            sort(mid + 1, hi)
            merge(lo, hi, 1)

    sort(0, n - 1)
    return pairs


_SORT16 = _oddeven_merge_sort_pairs(PEER_TOPK)


def _sort_desc(w):
    w = list(w)
    for a, b in _SORT16:
        hi, lo = jnp.maximum(w[a], w[b]), jnp.minimum(w[a], w[b])
        w[a], w[b] = hi, lo
    return w


def _merge_top(a, b):
    n = len(a)
    w = [jnp.maximum(a[i], b[n - 1 - i]) for i in range(n)]
    half = n // 2
    while half >= 1:
        for start in range(0, n, 2 * half):
            for i in range(start, start + half):
                hi, lo = jnp.maximum(w[i], w[i + half]), jnp.minimum(w[i], w[i + half])
                w[i], w[i + half] = hi, lo
        half //= 2
    return w


def _peer_topk_kernel(wk_ref, h2t_ref, ub_ref, nb_ref, vb_ref, rk_ref, sc_ref):
    H, K = PEER_HEADS, N_KEYS
    tb = h2t_ref.shape[1]
    h2t = h2t_ref[...]
    for a in range(2):
        for h in range(H):
            sc_ref[a, h] = _dot(wk_ref[a, h], h2t)

    sub = lax.broadcasted_iota(jnp.int32, (SUBLANES, LANES), 0)
    zeros = jnp.zeros((SUBLANES, LANES), F32)
    for lt in range(tb // LANES):
        ls = slice(lt * LANES, (lt + 1) * LANES)
        packed = []
        for a in range(2):
            acc = [None] * PEER_TOPK
            for h in range(H):
                w = [sc_ref[a, h, v * SUBLANES:(v + 1) * SUBLANES, ls] for v in range(K // SUBLANES)]
                w = _sort_desc(w)
                for shift in (4, 2, 1):
                    w = _merge_top(w, [pltpu.roll(x, shift, 0) for x in w])
                for i in range(PEER_TOPK):
                    acc[i] = w[i] if h == 0 else jnp.where(sub == h, w[i], acc[i])
            packed.append(acc)
        v1, v2 = packed
        rows = [[v1[a] + v2[b] for b in range(PEER_TOPK // (a + 1))] for a in range(PEER_TOPK)]
        neg = jnp.full((SUBLANES, LANES), -jnp.inf, F32)
        l0 = rows[0]
        l1 = _sort_desc(rows[1] + rows[2] + rows[4])
        l2 = _sort_desc(rows[3] + rows[5] + rows[6] + rows[7] + [rows[a][0] for a in range(8, 14)])
        l3 = [rows[14][0], rows[15][0]]
        l3 = [jnp.maximum(l3[0], l3[1]), jnp.minimum(l3[0], l3[1])] + [neg] * (PEER_TOPK - 2)
        vc = _merge_top(_merge_top(l0, l1), _merge_top(l2, l3))
        top, tau = vc[0], vc[PEER_TOPK - 1]
        zsum = zeros
        for i in range(PEER_TOPK):
            zsum = zsum + jnp.exp(vc[i] - top)
        inv_z = 1.0 / zsum
        n_sel = []
        for a in range(PEER_TOPK):
            cnt = zeros
            for cand in rows[a]:
                cnt = cnt + jnp.where(cand >= tau, 1.0, 0.0)
            n_sel.append(cnt)
        m1, m2 = v1[0], v2[0]
        for h in range(H):
            bc = lambda x: jnp.broadcast_to(x[h:h + 1, :], (SUBLANES, LANES))
            w1 = [bc(x) for x in v1]
            w2 = [bc(x) for x in v2]
            na = [bc(x) for x in n_sel]
            m1h, m2h, izh = bc(m1), bc(m2), bc(inv_z)
            ub, nb, vb, rk = [], [], [], []
            for v in range(K // SUBLANES):
                s1 = sc_ref[0, h, v * SUBLANES:(v + 1) * SUBLANES, ls]
                s2 = sc_ref[1, h, v * SUBLANES:(v + 1) * SUBLANES, ls]
                nbv, rkv = zeros, zeros
                for a in range(PEER_TOPK):
                    nbv = jnp.where(s1 == w1[a], na[a], nbv)
                    rkv = jnp.where(w2[a] > s2, float(a + 1), rkv)
                ub.append(jnp.exp(s1 - m1h) * izh)
                vb.append(jnp.exp(s2 - m2h))
                nb.append(nbv)
                rk.append(rkv)
            ub_ref[h, :, ls] = jnp.concatenate(ub, axis=0)
            nb_ref[h, :, ls] = jnp.concatenate(nb, axis=0)
            vb_ref[h, :, ls] = jnp.concatenate(vb, axis=0).astype(BF16)
            rk_ref[h, :, ls] = jnp.concatenate(rk, axis=0).astype(BF16)


def _peer_topk(wk, h2t, tb):
    D, T = h2t.shape
    H, K = PEER_HEADS, N_KEYS
    spec = pl.BlockSpec((H, K, tb), lambda i: (0, 0, i))
    return pl.pallas_call(
        _peer_topk_kernel,
        grid=(T // tb,),
        in_specs=[pl.BlockSpec(wk.shape, lambda i: (0, 0, 0, 0)), pl.BlockSpec((D, tb), lambda i: (0, i))],
        out_specs=[spec] * 4,
        out_shape=[jax.ShapeDtypeStruct((H, K, T), F32)] * 2 + [jax.ShapeDtypeStruct((H, K, T), BF16)] * 2,
        scratch_shapes=[pltpu.VMEM((2, H, K, tb), F32)],
        compiler_params=_cparams(("parallel",)),
        name="peer_topk",
    )(wk, h2t)


def _peer_main_kernel(h2t_ref, ue_ref, vet_ref, ub_ref, nb_ref, vb_ref, rk_ref, x2_ref, out_ref,
                      acc_ref, a_ref):
    c = pl.program_id(1)
    n_i = ue_ref.shape[0] // N_KEYS

    @pl.when(c == 0)
    def _():
        acc_ref[...] = jnp.zeros_like(acc_ref)

    h2t = h2t_ref[...]
    for ii in range(n_i):
        rs = slice(ii * N_KEYS, (ii + 1) * N_KEYS)
        pre = _dot(ue_ref[rs, :], h2t)
        act = (pre * (0.5 * lax.erf(pre * (2.0 ** -0.5)) + 0.5)).astype(BF16)
        gate = jnp.zeros(pre.shape, BF16)
        for h in range(PEER_HEADS):
            row = lambda ref: jnp.broadcast_to(ref[h, ii:ii + 1, :], pre.shape).astype(BF16)
            gate = gate + jnp.where(rk_ref[h] < row(nb_ref), row(ub_ref) * vb_ref[h], 0)
        a_ref[rs, :] = act * gate
    acc_ref[...] += _dot(vet_ref[...], a_ref[...])

    @pl.when(c == pl.num_programs(1) - 1)
    def _():
        out_ref[...] = x2_ref[...] + jnp.transpose(acc_ref[...])


def _peer_main(h2t, ue, vet, ub, nb, vb, rk, x2, tb, ec):
    D, T = h2t.shape
    E = ue.shape[0]
    H, K = PEER_HEADS, N_KEYS
    row = pl.BlockSpec((H, ec // K, tb), lambda t, c: (0, c, t))
    col = pl.BlockSpec((H, K, tb), lambda t, c: (0, 0, t))
    return pl.pallas_call(
        _peer_main_kernel,
        grid=(T // tb, E // ec),
        in_specs=[pl.BlockSpec((D, tb), lambda t, c: (0, t)),
                  pl.BlockSpec((ec, D), lambda t, c: (c, 0)),
                  pl.BlockSpec((D, ec), lambda t, c: (0, c)),
                  row, row, col, col,
                  pl.BlockSpec((tb, D), lambda t, c: (t, 0))],
        out_specs=pl.BlockSpec((tb, D), lambda t, c: (t, 0)),
        out_shape=jax.ShapeDtypeStruct((T, D), F32),
        scratch_shapes=[pltpu.VMEM((D, tb), F32), pltpu.VMEM((ec, tb), BF16)],
        compiler_params=_cparams(("parallel", "arbitrary")),
        name="peer_main",
    )(h2t, ue, vet, ub, nb, vb, rk, x2)


def _block_rows(T, want):
    tm = min(T, want)
    assert T % tm == 0
    return tm


def _layer(x2d, B, S, norm1_g, w_in, conv_w, conv_b, wq_m, wk_m, ig_b, fg_b, mh_norm_g, skip_m,
           qn_g, kn_g, rel_bias, w_out, norm2_g, w_query, sub_keys1, sub_keys2, expert_u, expert_v):
    T, D = x2d.shape
    o_vm, o_z, o_i, o_f, o_q = M_WIDTH, 2 * M_WIDTH, 3 * M_WIDTH, 3 * M_WIDTH + M_HEADS, 3 * M_WIDTH + 2 * M_HEADS
    wm = w_in[:, :o_i].astype(BF16)
    wa = w_in[:, o_q:].astype(BF16)
    wgt = jnp.transpose(w_in[:, o_i:o_q]).astype(BF16)
    gb = jnp.concatenate([ig_b, fg_b])[:, None]
    seg = np.arange(A_WIDTH) // A_HEAD_DIM
    bd = jnp.asarray(seg[:, None] == seg[None, :], BF16)
    qkg = jnp.stack([qn_g.reshape(-1), kn_g.reshape(-1)])
    tm = _block_rows(T, 512)
    uvz, qn, kn, va, grow = _inproj(x2d, norm1_g[None, :], wm, wa, wgt, gb, bd, qkg, tm)

    ym = _mlstm(uvz, grow, conv_w, conv_b[None, :], wq_m.astype(BF16), wk_m.astype(BF16),
                mh_norm_g.reshape(1, -1), skip_m.reshape(1, -1), B, S)

    ya = _dilattn(qn, kn, va, _attn_bias(rel_bias), B, S)

    x2, h2t = _outproj(x2d, ym, ya, w_out.astype(BF16), norm2_g[None, :], tm)

    keys = jnp.stack([sub_keys1, sub_keys2])
    wqt = jnp.transpose(w_query.reshape(D, PEER_HEADS, 2, PEER_QDIM // 2), (2, 1, 3, 0))
    wk = _peer_keys(keys, wqt)
    ub, nb, vb, rk = _peer_topk(wk, h2t, _block_rows(T, 256))
    ue = expert_u.astype(BF16)
    vet = jnp.transpose(expert_v).astype(BF16)
    return _peer_main(h2t, ue, vet, ub, nb, vb, rk, x2, _block_rows(T, 512), 2048)


def kernel(x, norm1_g, w_in, conv_w, conv_b, wq_m, wk_m, ig_b, fg_b, mh_norm_g, skip_m, qn_g, kn_g,
           rel_bias, w_out, norm2_g, w_query, sub_keys1, sub_keys2, expert_u, expert_v):
    B, S, D = x.shape
    assert S % DIL_PATTERNS[-1][0] == 0 and S % CHUNK == 0
    x2d = x.reshape(B * S, D)
    for l in range(norm1_g.shape[0]):
        x2d = _layer(x2d, B, S, norm1_g[l], w_in[l], conv_w[l], conv_b[l], wq_m[l], wk_m[l], ig_b[l],
                     fg_b[l], mh_norm_g[l], skip_m[l], qn_g[l], kn_g[l], rel_bias, w_out[l], norm2_g[l],
                     w_query[l], sub_keys1[l], sub_keys2[l], expert_u[l], expert_v[l])
    return x2d.reshape(B, S, D)
```

```python
import functools
import math

import numpy as np
import jax
import jax.numpy as jnp
from jax import lax
from jax.experimental import pallas as pl
from jax.experimental.pallas import tpu as pltpu

EPS = 1e-6
M_HEADS = 4
M_HEAD_DIM = 128
M_WIDTH = M_HEADS * M_HEAD_DIM
CONV_W = 4
CHUNK = 128
A_HEADS = 8
A_HEAD_DIM = 64
A_WIDTH = A_HEADS * A_HEAD_DIM
DIL_PATTERNS = ((128, 1), (512, 4), (2048, 16))
N_BUCKETS = 32
MAX_DISTANCE = 2048
PEER_HEADS = 8
N_KEYS = 128
PEER_QDIM = 256
PEER_TOPK = 16

LANES = 128
SUBLANES = 8
VMEM_LIMIT = 56 * 1024 * 1024

BF16 = jnp.bfloat16
F32 = jnp.float32
HI = lax.Precision.HIGHEST


def _cparams(sem):
    return pltpu.CompilerParams(dimension_semantics=sem, vmem_limit_bytes=VMEM_LIMIT)


def _dot(a, b):
    return jnp.dot(a, b, preferred_element_type=F32)


def _dot_nt(a, b):
    return lax.dot_general(a, b, (((1,), (1,)), ((), ())), preferred_element_type=F32)


def _dot_tn(a, b):
    return lax.dot_general(a, b, (((0,), (0,)), ((), ())), preferred_element_type=F32)


def _inproj_kernel(x_ref, g1_ref, wm_ref, wa_ref, wgt_ref, gb_ref, bd_ref, qkg_ref,
                   uvz_ref, q_ref, k_ref, v_ref, grow_ref):
    x = x_ref[...]
    h = x * lax.rsqrt(jnp.mean(x * x, axis=-1, keepdims=True) + EPS) * g1_ref[...]
    hb = h.astype(BF16)
    uvz_ref[...] = _dot(hb, wm_ref[...])
    qkv = _dot(hb, wa_ref[...])
    bd = bd_ref[...]

    def head_norm(t, g):
        ms = _dot((t * t).astype(BF16), bd) * (1.0 / A_HEAD_DIM)
        return t * lax.rsqrt(ms + EPS) * g

    q_ref[...] = head_norm(qkv[:, :A_WIDTH], qkg_ref[0:1, :])
    k_ref[...] = head_norm(qkv[:, A_WIDTH:2 * A_WIDTH], qkg_ref[1:2, :])
    v_ref[...] = qkv[:, 2 * A_WIDTH:]
    gr = _dot_nt(wgt_ref[...], hb) + gb_ref[...]
    lf = jnp.minimum(gr, 0.0) - jnp.log1p(jnp.exp(-jnp.abs(gr)))
    row = lax.broadcasted_iota(jnp.int32, gr.shape, 0)
    grow_ref[...] = jnp.where(row < M_HEADS, gr, lf)


def _inproj(x2d, g1, wm, wa, wgt, gb, bd, qkg, tm):
    T, D = x2d.shape
    full = lambda a: pl.BlockSpec(a.shape, lambda i: (0,) * a.ndim)
    return pl.pallas_call(
        _inproj_kernel,
        grid=(T // tm,),
        in_specs=[pl.BlockSpec((tm, D), lambda i: (i, 0)), full(g1), full(wm), full(wa), full(wgt),
                  full(gb), full(bd), full(qkg)],
        out_specs=[pl.BlockSpec((tm, 3 * M_WIDTH), lambda i: (i, 0)),
                   pl.BlockSpec((tm, A_WIDTH), lambda i: (i, 0)),
                   pl.BlockSpec((tm, A_WIDTH), lambda i: (i, 0)),
                   pl.BlockSpec((tm, A_WIDTH), lambda i: (i, 0)),
                   pl.BlockSpec((2 * M_HEADS, tm), lambda i: (0, i))],
        out_shape=[jax.ShapeDtypeStruct((T, 3 * M_WIDTH), F32),
                   jax.ShapeDtypeStruct((T, A_WIDTH), F32),
                   jax.ShapeDtypeStruct((T, A_WIDTH), F32),
                   jax.ShapeDtypeStruct((T, A_WIDTH), F32),
                   jax.ShapeDtypeStruct((2 * M_HEADS, T), F32)],
        compiler_params=_cparams(("parallel",)),
        name="inproj",
    )(x2d, g1, wm, wa, wgt, gb, bd, qkg)


def _mlstm_kernel(*refs):
    nb = (len(refs) - 8) // (2 + 2 * M_HEADS)
    uvz_all, grow_refs = refs[0], refs[1:1 + nb]
    cw_ref, cb_ref, wq_ref, wk_ref, ng_ref, sk_ref, ym_all = refs[1 + nb:8 + nb]
    ubuf_refs = refs[8 + nb:8 + 2 * nb]
    cst_refs = refs[8 + 2 * nb:8 + 2 * nb + nb * M_HEADS]
    mst_refs = refs[8 + 2 * nb + nb * M_HEADS:]
    L = CHUNK

    @pl.when(pl.program_id(0) == 0)
    def _():
        for r in ubuf_refs:
            r[0:SUBLANES, :] = jnp.zeros((SUBLANES, M_WIDTH), F32)
        for r in cst_refs + mst_refs:
            r[...] = jnp.zeros_like(r)

    rr = lax.broadcasted_iota(jnp.int32, (L, L), 0)
    cc = lax.broadcasted_iota(jnp.int32, (L, L), 1)
    causal = cc <= rr
    lane8 = lax.broadcasted_iota(jnp.int32, (2 * M_HEADS, L), 1)
    seqs = range(nb)
    chains = [(b, h) for b in seqs for h in range(M_HEADS)]
    cols = lambda h, off=0: slice(off + h * M_HEAD_DIM, off + (h + 1) * M_HEAD_DIM)

    cact, gr, b_rows, gcol, bcols = {}, {}, {}, {}, {}
    for b in seqs:
        uvz_ref, ubuf_ref = uvz_all.at[b], ubuf_refs[b]
        u = uvz_ref[:, 0:M_WIDTH]
        ubuf_ref[SUBLANES:SUBLANES + L, :] = u
        conv = cb_ref[...] + jnp.zeros((L, M_WIDTH), F32)
        for w in range(CONV_W):
            conv = conv + ubuf_ref[pl.ds(SUBLANES - (CONV_W - 1) + w, L), :] * cw_ref[w:w + 1, :]
        ubuf_ref[0:SUBLANES, :] = u[L - SUBLANES:, :]
        cact[b] = conv * jax.nn.sigmoid(conv)
        gr[b] = grow_refs[b][...]
    ch, qb, kb, kf, vaug = {}, {}, {}, {}, {}
    for c in chains:
        b, h = c
        ch[c] = cact[b][:, cols(h)]
        chb = ch[c].astype(BF16)
        qb[c] = _dot(chb, wq_ref[h]).astype(BF16)
        kf[c] = _dot(chb, wk_ref[h]) * (M_HEAD_DIM ** -0.5)
        kb[c] = kf[c].astype(BF16)
        v = uvz_all[b, :, cols(h, M_WIDTH)]
        vaug[c] = jnp.concatenate([v, jnp.ones((L, M_HEAD_DIM), F32)], axis=1).astype(BF16)

    for b in seqs:
        b_rows[b] = gr[b]
    shift = 1
    while shift < L:
        for b in seqs:
            b_rows[b] = b_rows[b] + jnp.where(lane8 >= shift, pltpu.roll(b_rows[b], shift, 1), 0.0)
        shift *= 2
    b_up = {b: pltpu.roll(b_rows[b], M_HEADS, 0) for b in seqs}
    rmax = {b: gr[b] - b_up[b] for b in seqs}
    shift = 1
    while shift < L:
        for b in seqs:
            rmax[b] = jnp.maximum(rmax[b], jnp.where(lane8 >= shift, pltpu.roll(rmax[b], shift, 1), -jnp.inf))
        shift *= 2
    pieces = {}
    for b in seqs:
        xs = jnp.concatenate([b_rows[b], gr[b], rmax[b] + b_up[b]], axis=0)
        hi = xs.astype(BF16)
        r1 = xs - hi.astype(F32)
        mid = r1.astype(BF16)
        pieces[b] = (hi, mid, (r1 - mid.astype(F32)).astype(BF16))
    sel_r = lax.broadcasted_iota(jnp.int32, (3 * 2 * M_HEADS, 3 * LANES), 0)
    sel_c = lax.broadcasted_iota(jnp.int32, (3 * 2 * M_HEADS, 3 * LANES), 1)

    b_c, li_c, b_last, m_prev, caug, m_t, d_mat, inter_w = {}, {}, {}, {}, {}, {}, {}, {}
    for c in chains:
        b, h = c
        want = jnp.where(sel_c < LANES, M_HEADS + h, jnp.where(sel_c < 2 * LANES, 2 * M_HEADS + h, 4 * M_HEADS + h))
        sel = (sel_r == want).astype(BF16)
        rep = _dot_tn(pieces[b][0], sel) + _dot_tn(pieces[b][1], sel) + _dot_tn(pieces[b][2], sel)
        b_c[c], li_c[c], row_max = rep[:, :LANES], rep[:, LANES:2 * LANES], rep[:, 2 * LANES:]
        b_r = b_rows[b][M_HEADS + h:M_HEADS + h + 1, :]
        li_r = gr[b][h:h + 1, :]
        b_last[c] = b_c[c][L - 1:L, :]
        m_prev[c] = mst_refs[b * M_HEADS + h][0:1, :]
        caug[c] = cst_refs[b * M_HEADS + h][...]
        log_d = b_c[c] - b_r + li_r
        inter = b_c[c] + m_prev[c]
        m_t[c] = jnp.maximum(inter, row_max)
        d_mat[c] = jnp.where(causal, jnp.exp(log_d - m_t[c]), 0.0)
        inter_w[c] = jnp.exp(inter - m_t[c])

    twice = lambda x: jnp.concatenate([x, x], axis=1)
    s = {c: _dot_nt(qb[c], kb[c]) * d_mat[c] for c in chains}
    qc = {c: _dot(qb[c], caug[c].astype(BF16)) for c in chains}
    nd = {c: _dot(s[c].astype(BF16), vaug[c]) + twice(inter_w[c]) * qc[c] for c in chains}

    for c in chains:
        b, h = c
        g_c = b_last[c] - b_c[c] + li_c[c]
        m_loc = jnp.max(g_c, axis=0, keepdims=True)
        kw = (jnp.exp(g_c - m_loc) * kf[c]).astype(BF16)
        c_new = _dot_tn(kw, vaug[c])
        m_new = jnp.maximum(b_last[c] + m_prev[c], m_loc)
        a = jnp.exp(b_last[c] + m_prev[c] - m_new)
        bb = jnp.exp(m_loc - m_new)
        cst_refs[b * M_HEADS + h][...] = twice(a) * caug[c] + twice(bb) * c_new
        mst_refs[b * M_HEADS + h][...] = jnp.broadcast_to(m_new, (SUBLANES, LANES))

    mean_mat = jnp.full((M_HEAD_DIM, M_HEAD_DIM), 1.0 / M_HEAD_DIM, BF16)
    for c in chains:
        b, h = c
        num, den = nd[c][:, :M_HEAD_DIM], nd[c][:, M_HEAD_DIM:]
        hval = num / jnp.maximum(jnp.abs(den), jnp.exp(-m_t[c]))
        hn = hval * lax.rsqrt(_dot((hval * hval).astype(BF16), mean_mat) + EPS) * ng_ref[:, cols(h)]
        hn = hn + sk_ref[:, cols(h)] * ch[c]
        z = uvz_all[b, :, cols(h, 2 * M_WIDTH)]
        ym_all[b, :, cols(h)] = jax.nn.sigmoid(z) * hn


def _mlstm(uvz, grow, cw, cb, wq, wk, ng, sk, B, S):
    nc = S // CHUNK
    full = lambda a: pl.BlockSpec(a.shape, lambda c: (0,) * a.ndim)
    gate_specs = [pl.BlockSpec((2 * M_HEADS, CHUNK), lambda c, b=b: (0, b * nc + c)) for b in range(B)]
    ym = pl.pallas_call(
        _mlstm_kernel,
        grid=(nc,),
        in_specs=[pl.BlockSpec((B, CHUNK, 3 * M_WIDTH), lambda c: (0, c, 0))] + gate_specs
                 + [full(cw), full(cb), full(wq), full(wk), full(ng), full(sk)],
        out_specs=pl.BlockSpec((B, CHUNK, M_WIDTH), lambda c: (0, c, 0)),
        out_shape=jax.ShapeDtypeStruct((B, S, M_WIDTH), F32),
        scratch_shapes=([pltpu.VMEM((SUBLANES + CHUNK, M_WIDTH), F32)] * B
                        + [pltpu.VMEM((M_HEAD_DIM, 2 * M_HEAD_DIM), F32)] * (B * M_HEADS)
                        + [pltpu.VMEM((SUBLANES, LANES), F32)] * (B * M_HEADS)),
        compiler_params=_cparams(("arbitrary",)),
        name="mlstm",
    )(uvz.reshape(B, S, 3 * M_WIDTH), *([grow] * B), cw, cb, wq, wk, ng, sk)
    return ym.reshape(B * S, M_WIDTH)


ATT_BLK = DIL_PATTERNS[0][0] // DIL_PATTERNS[0][1]
ATT_SPAN = DIL_PATTERNS[-1][0]


ATT_R = DIL_PATTERNS[1][1]
assert [d for _, d in DIL_PATTERNS] == [1, ATT_R, ATT_R * ATT_R]
ATT_UNROLL = 8


def _dilattn_kernel(q_ref, kp_ref, kc_ref, vp_ref, vc_ref, bias_ref, o_ref,
                    q4_ref, k4_ref, v4_ref, k1_ref, v1_ref, a1_ref, a4_ref):
    blk, span, R = ATT_BLK, ATT_SPAN, ATT_R
    sub = span // R
    n = pl.program_id(1)
    scale = A_HEAD_DIM ** -0.5
    for r in range(R):
        res = pl.ds(r, sub, stride=R)
        q4_ref[r] = q_ref[res, :] * scale
        k4_ref[r, 0:sub] = kp_ref[res, :]
        k4_ref[r, sub:] = kc_ref[res, :]
        v4_ref[r, 0:sub] = vp_ref[res, :]
        v4_ref[r, sub:] = vc_ref[res, :]
    k1_ref[0:blk] = kp_ref[span - blk:, :]
    k1_ref[blk:] = kc_ref[...]
    v1_ref[0:blk] = vp_ref[span - blk:, :]
    v1_ref[blk:] = vc_ref[...]
    first_q = lax.broadcasted_iota(jnp.int32, (blk, LANES), 1) < A_HEAD_DIM

    def attend(q, kk, vv, bias_sel):
        kb, vb = kk.astype(BF16), vv.astype(BF16)
        ms, ls, os_ = [], [], []
        for s_ in range(2):
            qh = jnp.where(first_q, q, 0.0) if s_ == 0 else jnp.where(first_q, 0.0, q)
            s = _dot_nt(qh.astype(BF16), kb) + bias_sel(s_)
            m = jnp.max(s, axis=1, keepdims=True)
            e = jnp.exp(s - m)
            ms.append(m)
            ls.append(jnp.sum(e, axis=1, keepdims=True))
            os_.append(_dot(e.astype(BF16), vb))
        return (jnp.where(first_q, ms[0], ms[1]), jnp.where(first_q, ls[0], ls[1]),
                jnp.where(first_q, os_[0], os_[1]))

    def unit_d1(m, carry):
        base = pl.multiple_of(m * blk, blk)
        first = jnp.logical_and(n == 0, m == 0).astype(jnp.int32)
        m_c, l_c, o_c = attend(q_ref[pl.ds(base, blk), :] * scale, k1_ref[pl.ds(base, 2 * blk), :],
                               v1_ref[pl.ds(base, 2 * blk), :], lambda s_: bias_ref[0, first, s_])
        a1_ref[0, pl.ds(base, blk), :] = m_c
        a1_ref[1, pl.ds(base, blk), :] = l_c
        a1_ref[2, pl.ds(base, blk), :] = o_c
        return carry

    def unit_d4(u, carry):
        m4, r = u >> (R.bit_length() - 1), u & (R - 1)
        off = pl.multiple_of(m4 * blk, blk)
        first = jnp.logical_and(n == 0, m4 == 0).astype(jnp.int32)
        keys = pl.ds(sub - blk + off, 2 * blk)
        m_c, l_c, o_c = attend(q4_ref[r, pl.ds(off, blk), :], k4_ref[r, keys, :], v4_ref[r, keys, :],
                               lambda s_: bias_ref[1, first, s_])
        a4_ref[0, r, pl.ds(off, blk), :] = m_c
        a4_ref[1, r, pl.ds(off, blk), :] = l_c
        a4_ref[2, r, pl.ds(off, blk), :] = o_c
        return carry

    def unit_d16(u, carry):
        r, rp = u & (R - 1), u >> (R.bit_length() - 1)
        first = (n == 0).astype(jnp.int32)
        own, cur = pl.ds(rp, blk, stride=R), pl.ds(sub + rp, blk, stride=R)
        kk = jnp.concatenate([k4_ref[r, own, :], k4_ref[r, cur, :]], axis=0)
        vv = jnp.concatenate([v4_ref[r, own, :], v4_ref[r, cur, :]], axis=0)
        m_c, l_c, o_c = attend(q4_ref[r, own, :], kk, vv, lambda s_: bias_ref[2, first, s_])
        m_o = a4_ref[0, r, own, :]
        m_n = jnp.maximum(m_o, m_c)
        a_o, a_c = jnp.exp(m_o - m_n), jnp.exp(m_c - m_n)
        a4_ref[0, r, own, :] = m_n
        a4_ref[1, r, own, :] = a_o * a4_ref[1, r, own, :] + a_c * l_c
        a4_ref[2, r, own, :] = a_o * a4_ref[2, r, own, :] + a_c * o_c
        return carry

    units = span // blk
    lax.fori_loop(0, units, unit_d1, 0, unroll=ATT_UNROLL)
    lax.fori_loop(0, units, unit_d4, 0, unroll=ATT_UNROLL)
    lax.fori_loop(0, units, unit_d16, 0, unroll=ATT_UNROLL)

    for r in range(R):
        res = pl.ds(r, sub, stride=R)
        m1, m4 = a1_ref[0, res, :], a4_ref[0, r]
        m_n = jnp.maximum(m1, m4)
        e1, e4 = jnp.exp(m1 - m_n), jnp.exp(m4 - m_n)
        o_ref[res, :] = ((e1 * a1_ref[2, res, :] + e4 * a4_ref[2, r])
                         / (e1 * a1_ref[1, res, :] + e4 * a4_ref[1, r]))


def _dilattn(q, k, v, bias, B, S):
    span, blk, R = ATT_SPAN, ATT_BLK, ATT_R
    ns = S // span
    T = B * S
    cur = pl.BlockSpec((span, LANES), lambda b, n, hp: (b * ns + n, hp))
    prev = pl.BlockSpec((span, LANES), lambda b, n, hp: (b * ns + jnp.maximum(n - 1, 0), hp))
    return pl.pallas_call(
        _dilattn_kernel,
        grid=(B, ns, A_HEADS // 2),
        in_specs=[cur, prev, cur, prev, cur,
                  pl.BlockSpec((len(DIL_PATTERNS), 2, 2, blk, 2 * blk), lambda b, n, hp: (0, 0, hp, 0, 0))],
        out_specs=cur,
        out_shape=jax.ShapeDtypeStruct((T, A_WIDTH), F32),
        scratch_shapes=[pltpu.VMEM((R, span // R, LANES), F32),
                        pltpu.VMEM((R, 2 * span // R, LANES), F32),
                        pltpu.VMEM((R, 2 * span // R, LANES), F32),
                        pltpu.VMEM((blk + span, LANES), F32),
                        pltpu.VMEM((blk + span, LANES), F32),
                        pltpu.VMEM((3, span, LANES), F32),
                        pltpu.VMEM((3, R, span // R, LANES), F32)],
        compiler_params=_cparams(("parallel", "parallel", "parallel")),
        name="dilattn",
    )(q, k, k, v, v, bias)


def _attn_bias_kernel(bkt_ref, rb_ref, out_ref):
    h = pl.program_id(1)
    bkt = bkt_ref[...]
    acc = jnp.full(bkt.shape, -jnp.inf, F32)
    for kk in range(N_BUCKETS):
        acc = jnp.where(bkt == kk, rb_ref[kk, h], acc)
    col = lax.broadcasted_iota(jnp.int32, bkt.shape, 1)
    out_ref[0] = acc
    out_ref[1] = jnp.where(col < ATT_BLK, -jnp.inf, acc)


def _attn_bias(rel_bias):
    blk = ATT_BLK
    i = jnp.arange(blk)[:, None]
    j = jnp.arange(2 * blk)[None, :]
    steps = i + blk - j
    band = (steps >= 0) & (steps <= blk)
    bkt = jnp.stack([jnp.where(band, _t5_bucket(jnp.maximum(steps, 0) * d), -1) for _, d in DIL_PATTERNS])
    P = len(DIL_PATTERNS)
    return pl.pallas_call(
        _attn_bias_kernel,
        grid=(P, A_HEADS),
        in_specs=[pl.BlockSpec((None, blk, 2 * blk), lambda p, h: (p, 0, 0)),
                  pl.BlockSpec(memory_space=pltpu.SMEM)],
        out_specs=pl.BlockSpec((None, 2, None, blk, 2 * blk), lambda p, h: (p, 0, h, 0, 0)),
        out_shape=jax.ShapeDtypeStruct((P, 2, A_HEADS, blk, 2 * blk), F32),
        compiler_params=_cparams(("parallel", "parallel")),
        name="attn_bias",
    )(bkt.astype(jnp.int32), rel_bias)


def _t5_bucket(dist):
    max_exact = N_BUCKETS // 2
    nf = jnp.maximum(dist, 1).astype(F32)
    large = max_exact + (jnp.log(nf / max_exact) / np.log(MAX_DISTANCE / max_exact)
                         * (N_BUCKETS - max_exact)).astype(jnp.int32)
    large = jnp.minimum(large, N_BUCKETS - 1)
    return jnp.where(dist < max_exact, dist, large)


def _outproj_kernel(x_ref, ym_ref, ya_ref, wo_ref, g2_ref, x2_ref, h2t_ref):
    y = (_dot(ym_ref[...].astype(BF16), wo_ref[0:M_WIDTH, :])
         + _dot(ya_ref[...].astype(BF16), wo_ref[M_WIDTH:, :]))
    x2 = x_ref[...] + y
    x2_ref[...] = x2
    h2 = x2 * lax.rsqrt(jnp.mean(x2 * x2, axis=-1, keepdims=True) + EPS) * g2_ref[...]
    h2t_ref[...] = jnp.transpose(h2).astype(BF16)


def _outproj(x2d, ym, ya, wo, g2, tm):
    T, D = x2d.shape
    half = pl.BlockSpec((tm, M_WIDTH), lambda i: (i, 0))
    full = lambda a: pl.BlockSpec(a.shape, lambda i: (0,) * a.ndim)
    return pl.pallas_call(
        _outproj_kernel,
        grid=(T // tm,),
        in_specs=[pl.BlockSpec((tm, D), lambda i: (i, 0)), half, half, full(wo), full(g2)],
        out_specs=[pl.BlockSpec((tm, D), lambda i: (i, 0)), pl.BlockSpec((D, tm), lambda i: (0, i))],
        out_shape=[jax.ShapeDtypeStruct((T, D), F32), jax.ShapeDtypeStruct((D, T), BF16)],
        compiler_params=_cparams(("parallel",)),
        name="outproj",
    )(x2d, ym, ya, wo, g2)


def _peer_keys_kernel(keys_ref, wqt_ref, out_ref):
    out_ref[...] = jnp.dot(keys_ref[...], wqt_ref[...], preferred_element_type=F32,
                           precision=HI).astype(out_ref.dtype)


def _peer_keys(keys, wqt):
    _, H, K, C = keys.shape
    D = wqt.shape[-1]
    return pl.pallas_call(
        _peer_keys_kernel,
        grid=(2, H),
        in_specs=[pl.BlockSpec((None, None, K, C), lambda a, h: (a, h, 0, 0)),
                  pl.BlockSpec((None, None, C, D), lambda a, h: (a, h, 0, 0))],
        out_specs=pl.BlockSpec((None, None, K, D), lambda a, h: (a, h, 0, 0)),
        out_shape=jax.ShapeDtypeStruct((2, H, K, D), BF16),
        compiler_params=_cparams(("parallel", "parallel")),
        name="peer_keys",
    )(keys, wqt)


def _oddeven_merge_sort_pairs(n):
    pairs = []

    def merge(lo, hi, r):
        step = r * 2
        if step < hi - lo:
            merge(lo, hi, step)
            merge(lo + r, hi, step)
            for i in range(lo + r, hi - r, step):
                pairs.append((i, i + r))
        else:
            pairs.append((lo, lo + r))

    def sort(lo, hi):
        if hi - lo >= 1:
            mid = lo + (hi - lo) // 2
            sort(lo, mid)
            sort(mid + 1, hi)
            merge(lo, hi, 1)

    sort(0, n - 1)
    return pairs


_SORT16 = _oddeven_merge_sort_pairs(PEER_TOPK)


def _sort_desc(w):
    w = list(w)
    for a, b in _SORT16:
        hi, lo = jnp.maximum(w[a], w[b]), jnp.minimum(w[a], w[b])
        w[a], w[b] = hi, lo
    return w


def _merge_top(a, b):
    n = len(a)
    w = [jnp.maximum(a[i], b[n - 1 - i]) for i in range(n)]
    half = n // 2
    while half >= 1:
        for start in range(0, n, 2 * half):
            for i in range(start, start + half):
                hi, lo = jnp.maximum(w[i], w[i + half]), jnp.minimum(w[i], w[i + half])
                w[i], w[i + half] = hi, lo
        half //= 2
    return w


def _peer_topk_kernel(wk_ref, h2t_ref, ub_ref, nb_ref, vb_ref, rk_ref, sc_ref):
    H, K = PEER_HEADS, N_KEYS
    tb = h2t_ref.shape[1]
    h2t = h2t_ref[...]
    for a in range(2):
        for h in range(H):
            sc_ref[a, h] = _dot(wk_ref[a, h], h2t)

    sub = lax.broadcasted_iota(jnp.int32, (SUBLANES, LANES), 0)
    zeros = jnp.zeros((SUBLANES, LANES), F32)
    for lt in range(tb // LANES):
        ls = slice(lt * LANES, (lt + 1) * LANES)
        packed = []
        for a in range(2):
            acc = [None] * PEER_TOPK
            for h in range(H):
                w = [sc_ref[a, h, v * SUBLANES:(v + 1) * SUBLANES, ls] for v in range(K // SUBLANES)]
                w = _sort_desc(w)
                for shift in (4, 2, 1):
                    w = _merge_top(w, [pltpu.roll(x, shift, 0) for x in w])
                for i in range(PEER_TOPK):
                    acc[i] = w[i] if h == 0 else jnp.where(sub == h, w[i], acc[i])
            packed.append(acc)
        v1, v2 = packed
        rows = [[v1[a] + v2[b] for b in range(PEER_TOPK // (a + 1))] for a in range(PEER_TOPK)]
        neg = jnp.full((SUBLANES, LANES), -jnp.inf, F32)
        l0 = rows[0]
        l1 = _sort_desc(rows[1] + rows[2] + rows[4])
        l2 = _sort_desc(rows[3] + rows[5] + rows[6] + rows[7] + [rows[a][0] for a in range(8, 14)])
        l3 = [rows[14][0], rows[15][0]]
        l3 = [jnp.maximum(l3[0], l3[1]), jnp.minimum(l3[0], l3[1])] + [neg] * (PEER_TOPK - 2)
        vc = _merge_top(_merge_top(l0, l1), _merge_top(l2, l3))
        top, tau = vc[0], vc[PEER_TOPK - 1]
        zsum = zeros
        for i in range(PEER_TOPK):
            zsum = zsum + jnp.exp(vc[i] - top)
        inv_z = 1.0 / zsum
        n_sel = []
        for a in range(PEER_TOPK):
            cnt = zeros
            for cand in rows[a]:
                cnt = cnt + jnp.where(cand >= tau, 1.0, 0.0)
            n_sel.append(cnt)
        m1, m2 = v1[0], v2[0]
        for h in range(H):
            bc = lambda x: jnp.broadcast_to(x[h:h + 1, :], (SUBLANES, LANES))
            w1 = [bc(x) for x in v1]
            w2 = [bc(x) for x in v2]
            na = [bc(x) for x in n_sel]
            m1h, m2h, izh = bc(m1), bc(m2), bc(inv_z)
            ub, nb, vb, rk = [], [], [], []
            for v in range(K // SUBLANES):
                s1 = sc_ref[0, h, v * SUBLANES:(v + 1) * SUBLANES, ls]
                s2 = sc_ref[1, h, v * SUBLANES:(v + 1) * SUBLANES, ls]
                nbv, rkv = zeros, zeros
                for a in range(PEER_TOPK):
                    nbv = jnp.where(s1 == w1[a], na[a], nbv)
                    rkv = jnp.where(w2[a] > s2, float(a + 1), rkv)
                ub.append(jnp.exp(s1 - m1h) * izh * 0.5)
                vb.append(jnp.exp(s2 - m2h))
                nb.append(nbv)
                rk.append(rkv)
            ub_ref[h, :, ls] = jnp.concatenate(ub, axis=0)
            nb_ref[h, :, ls] = jnp.concatenate(nb, axis=0)
            vb_ref[h, :, ls] = jnp.concatenate(vb, axis=0).astype(BF16)
            rk_ref[h, :, ls] = jnp.concatenate(rk, axis=0).astype(BF16)


def _peer_topk(wk, h2t, tb):
    D, T = h2t.shape
    H, K = PEER_HEADS, N_KEYS
    spec = pl.BlockSpec((H, K, tb), lambda i: (0, 0, i))
    return pl.pallas_call(
        _peer_topk_kernel,
        grid=(T // tb,),
        in_specs=[pl.BlockSpec(wk.shape, lambda i: (0, 0, 0, 0)), pl.BlockSpec((D, tb), lambda i: (0, i))],
        out_specs=[spec] * 4,
        out_shape=[jax.ShapeDtypeStruct((H, K, T), F32)] * 2 + [jax.ShapeDtypeStruct((H, K, T), BF16)] * 2,
        scratch_shapes=[pltpu.VMEM((2, H, K, tb), F32)],
        compiler_params=_cparams(("parallel",)),
        name="peer_topk",
    )(wk, h2t)


def _peer_main_kernel(h2t_ref, ue_ref, vet_ref, ub_ref, nb_ref, vb_ref, rk_ref, x2_ref, out_ref,
                      acc_ref, a_ref):
    c = pl.program_id(1)
    n_i = ue_ref.shape[0] // N_KEYS

    @pl.when(c == 0)
    def _():
        acc_ref[...] = jnp.zeros_like(acc_ref)

    h2t = h2t_ref[...]
    for ii in range(n_i):
        rs = slice(ii * N_KEYS, (ii + 1) * N_KEYS)
        pre = _dot(ue_ref[rs, :], h2t)
        act = (pre * (lax.erf(pre * (2.0 ** -0.5)) + 1.0)).astype(BF16)
        gate = jnp.zeros(pre.shape, BF16)
        for h in range(PEER_HEADS):
            row = lambda ref: jnp.broadcast_to(ref[h, ii:ii + 1, :], pre.shape).astype(BF16)
            gate = gate + jnp.where(rk_ref[h] < row(nb_ref), row(ub_ref) * vb_ref[h], 0)
        a_ref[rs, :] = act * gate
    acc_ref[...] += _dot(vet_ref[...], a_ref[...])

    @pl.when(c == pl.num_programs(1) - 1)
    def _():
        out_ref[...] = x2_ref[...] + jnp.transpose(acc_ref[...])


def _peer_main(h2t, ue, vet, ub, nb, vb, rk, x2, tb, ec):
    D, T = h2t.shape
    E = ue.shape[0]
    H, K = PEER_HEADS, N_KEYS
    row = pl.BlockSpec((H, ec // K, tb), lambda t, c: (0, c, t))
    col = pl.BlockSpec((H, K, tb), lambda t, c: (0, 0, t))
    return pl.pallas_call(
        _peer_main_kernel,
        grid=(T // tb, E // ec),
        in_specs=[pl.BlockSpec((D, tb), lambda t, c: (0, t)),
                  pl.BlockSpec((ec, D), lambda t, c: (c, 0)),
                  pl.BlockSpec((D, ec), lambda t, c: (0, c)),
                  row, row, col, col,
                  pl.BlockSpec((tb, D), lambda t, c: (t, 0))],
        out_specs=pl.BlockSpec((tb, D), lambda t, c: (t, 0)),
        out_shape=jax.ShapeDtypeStruct((T, D), F32),
        scratch_shapes=[pltpu.VMEM((D, tb), F32), pltpu.VMEM((ec, tb), BF16)],
        compiler_params=_cparams(("parallel", "arbitrary")),
        name="peer_main",
    )(h2t, ue, vet, ub, nb, vb, rk, x2)


def _block_rows(T, want):
    tm = min(T, want)
    assert T % tm == 0
    return tm


def _layer(x2d, B, S, norm1_g, w_in, conv_w, conv_b, wq_m, wk_m, ig_b, fg_b, mh_norm_g, skip_m,
           qn_g, kn_g, rel_bias, w_out, norm2_g, w_query, sub_keys1, sub_keys2, expert_u, expert_v):
    T, D = x2d.shape
    o_vm, o_z, o_i, o_f, o_q = M_WIDTH, 2 * M_WIDTH, 3 * M_WIDTH, 3 * M_WIDTH + M_HEADS, 3 * M_WIDTH + 2 * M_HEADS
    wm = w_in[:, :o_i].astype(BF16)
    wa = w_in[:, o_q:].astype(BF16)
    wgt = jnp.transpose(w_in[:, o_i:o_q]).astype(BF16)
    gb = jnp.concatenate([ig_b, fg_b])[:, None]
    seg = np.arange(A_WIDTH) // A_HEAD_DIM
    bd = jnp.asarray(seg[:, None] == seg[None, :], BF16)
    qkg = jnp.stack([qn_g.reshape(-1), kn_g.reshape(-1)])
    tm = _block_rows(T, 512)
    uvz, qn, kn, va, grow = _inproj(x2d, norm1_g[None, :], wm, wa, wgt, gb, bd, qkg, tm)

    ym = _mlstm(uvz, grow, conv_w, conv_b[None, :], wq_m.astype(BF16), wk_m.astype(BF16),
                mh_norm_g.reshape(1, -1), skip_m.reshape(1, -1), B, S)

    ya = _dilattn(qn, kn, va, _attn_bias(rel_bias), B, S)

    x2, h2t = _outproj(x2d, ym, ya, w_out.astype(BF16), norm2_g[None, :], tm)

    keys = jnp.stack([sub_keys1, sub_keys2])
    wqt = jnp.transpose(w_query.reshape(D, PEER_HEADS, 2, PEER_QDIM // 2), (2, 1, 3, 0))
    wk = _peer_keys(keys, wqt)
    ub, nb, vb, rk = _peer_topk(wk, h2t, _block_rows(T, 256))
    ue = expert_u.astype(BF16)
    vet = jnp.transpose(expert_v).astype(BF16)
    return _peer_main(h2t, ue, vet, ub, nb, vb, rk, x2, _block_rows(T, 512), 2048)


def kernel(x, norm1_g, w_in, conv_w, conv_b, wq_m, wk_m, ig_b, fg_b, mh_norm_g, skip_m, qn_g, kn_g,
           rel_bias, w_out, norm2_g, w_query, sub_keys1, sub_keys2, expert_u, expert_v):
    B, S, D = x.shape
    assert S % DIL_PATTERNS[-1][0] == 0 and S % CHUNK == 0
    x2d = x.reshape(B * S, D)
    for l in range(norm1_g.shape[0]):
        x2d = _layer(x2d, B, S, norm1_g[l], w_in[l], conv_w[l], conv_b[l], wq_m[l], wk_m[l], ig_b[l],
                     fg_b[l], mh_norm_g[l], skip_m[l], qn_g[l], kn_g[l], rel_bias, w_out[l], norm2_g[l],
                     w_query[l], sub_keys1[l], sub_keys2[l], expert_u[l], expert_v[l])
    return x2d.reshape(B, S, D)
```

```python
import functools
import math

import numpy as np
import jax
import jax.numpy as jnp
from jax import lax
from jax.experimental import pallas as pl
from jax.experimental.pallas import tpu as pltpu

EPS = 1e-6
M_HEADS = 4
M_HEAD_DIM = 128
M_WIDTH = M_HEADS * M_HEAD_DIM
CONV_W = 4
CHUNK = 128
A_HEADS = 8
A_HEAD_DIM = 64
A_WIDTH = A_HEADS * A_HEAD_DIM
DIL_PATTERNS = ((128, 1), (512, 4), (2048, 16))
N_BUCKETS = 32
MAX_DISTANCE = 2048
PEER_HEADS = 8
N_KEYS = 128
PEER_QDIM = 256
PEER_TOPK = 16

LANES = 128
SUBLANES = 8
VMEM_LIMIT = 56 * 1024 * 1024

BF16 = jnp.bfloat16
F32 = jnp.float32
HI = lax.Precision.HIGHEST


def _cparams(sem):
    return pltpu.CompilerParams(dimension_semantics=sem, vmem_limit_bytes=VMEM_LIMIT)


def _dot(a, b):
    return jnp.dot(a, b, preferred_element_type=F32)


def _dot_nt(a, b):
    return lax.dot_general(a, b, (((1,), (1,)), ((), ())), preferred_element_type=F32)


def _dot_tn(a, b):
    return lax.dot_general(a, b, (((0,), (0,)), ((), ())), preferred_element_type=F32)


def _inproj_kernel(x_ref, g1_ref, wm_ref, wa_ref, wgt_ref, gb_ref, bd_ref, qkg_ref,
                   uvz_ref, q_ref, k_ref, v_ref, grow_ref):
    x = x_ref[...]
    h = x * lax.rsqrt(jnp.mean(x * x, axis=-1, keepdims=True) + EPS) * g1_ref[...]
    hb = h.astype(BF16)
    uvz_ref[...] = _dot(hb, wm_ref[...])
    qkv = _dot(hb, wa_ref[...])
    bd = bd_ref[...]

    def head_norm(t, g):
        ms = _dot((t * t).astype(BF16), bd) * (1.0 / A_HEAD_DIM)
        return t * lax.rsqrt(ms + EPS) * g

    q_ref[...] = head_norm(qkv[:, :A_WIDTH], qkg_ref[0:1, :])
    k_ref[...] = head_norm(qkv[:, A_WIDTH:2 * A_WIDTH], qkg_ref[1:2, :])
    v_ref[...] = qkv[:, 2 * A_WIDTH:]
    gr = _dot_nt(wgt_ref[...], hb) + gb_ref[...]
    lf = jnp.minimum(gr, 0.0) - jnp.log1p(jnp.exp(-jnp.abs(gr)))
    row = lax.broadcasted_iota(jnp.int32, gr.shape, 0)
    grow_ref[...] = jnp.where(row < M_HEADS, gr, lf)


def _inproj(x2d, g1, wm, wa, wgt, gb, bd, qkg, tm):
    T, D = x2d.shape
    full = lambda a: pl.BlockSpec(a.shape, lambda i: (0,) * a.ndim)
    return pl.pallas_call(
        _inproj_kernel,
        grid=(T // tm,),
        in_specs=[pl.BlockSpec((tm, D), lambda i: (i, 0)), full(g1), full(wm), full(wa), full(wgt),
                  full(gb), full(bd), full(qkg)],
        out_specs=[pl.BlockSpec((tm, 3 * M_WIDTH), lambda i: (i, 0)),
                   pl.BlockSpec((tm, A_WIDTH), lambda i: (i, 0)),
                   pl.BlockSpec((tm, A_WIDTH), lambda i: (i, 0)),
                   pl.BlockSpec((tm, A_WIDTH), lambda i: (i, 0)),
                   pl.BlockSpec((2 * M_HEADS, tm), lambda i: (0, i))],
        out_shape=[jax.ShapeDtypeStruct((T, 3 * M_WIDTH), F32),
                   jax.ShapeDtypeStruct((T, A_WIDTH), F32),
                   jax.ShapeDtypeStruct((T, A_WIDTH), F32),
                   jax.ShapeDtypeStruct((T, A_WIDTH), F32),
                   jax.ShapeDtypeStruct((2 * M_HEADS, T), F32)],
        compiler_params=_cparams(("parallel",)),
        name="inproj",
    )(x2d, g1, wm, wa, wgt, gb, bd, qkg)


def _mlstm_kernel(*refs):
    nb = (len(refs) - 8) // (2 + 2 * M_HEADS)
    uvz_all, grow_refs = refs[0], refs[1:1 + nb]
    cw_ref, cb_ref, wq_ref, wk_ref, ng_ref, sk_ref, ym_all = refs[1 + nb:8 + nb]
    ubuf_refs = refs[8 + nb:8 + 2 * nb]
    cst_refs = refs[8 + 2 * nb:8 + 2 * nb + nb * M_HEADS]
    mst_refs = refs[8 + 2 * nb + nb * M_HEADS:]
    L = CHUNK

    @pl.when(pl.program_id(0) == 0)
    def _():
        for r in ubuf_refs:
            r[0:SUBLANES, :] = jnp.zeros((SUBLANES, M_WIDTH), F32)
        for r in cst_refs + mst_refs:
            r[...] = jnp.zeros_like(r)

    rr = lax.broadcasted_iota(jnp.int32, (L, L), 0)
    cc = lax.broadcasted_iota(jnp.int32, (L, L), 1)
    causal = cc <= rr
    lane8 = lax.broadcasted_iota(jnp.int32, (2 * M_HEADS, L), 1)
    seqs = range(nb)
    chains = [(b, h) for b in seqs for h in range(M_HEADS)]
    cols = lambda h, off=0: slice(off + h * M_HEAD_DIM, off + (h + 1) * M_HEAD_DIM)

    cact, gr, b_rows, gcol, bcols = {}, {}, {}, {}, {}
    for b in seqs:
        uvz_ref, ubuf_ref = uvz_all.at[b], ubuf_refs[b]
        u = uvz_ref[:, 0:M_WIDTH]
        ubuf_ref[SUBLANES:SUBLANES + L, :] = u
        conv = cb_ref[...] + jnp.zeros((L, M_WIDTH), F32)
        for w in range(CONV_W):
            conv = conv + ubuf_ref[pl.ds(SUBLANES - (CONV_W - 1) + w, L), :] * cw_ref[w:w + 1, :]
        ubuf_ref[0:SUBLANES, :] = u[L - SUBLANES:, :]
        cact[b] = conv * jax.nn.sigmoid(conv)
        gr[b] = grow_refs[b][...]
    ch, qb, kb, kf, vaug = {}, {}, {}, {}, {}
    for c in chains:
        b, h = c
        ch[c] = cact[b][:, cols(h)]
        chb = ch[c].astype(BF16)
        qb[c] = _dot(chb, wq_ref[h]).astype(BF16)
        kf[c] = _dot(chb, wk_ref[h]) * (M_HEAD_DIM ** -0.5)
        kb[c] = kf[c].astype(BF16)
        v = uvz_all[b, :, cols(h, M_WIDTH)]
        vaug[c] = jnp.concatenate([v, jnp.ones((L, M_HEAD_DIM), F32)], axis=1).astype(BF16)

    for b in seqs:
        b_rows[b] = gr[b]
    shift = 1
    while shift < L:
        for b in seqs:
            b_rows[b] = b_rows[b] + jnp.where(lane8 >= shift, pltpu.roll(b_rows[b], shift, 1), 0.0)
        shift *= 2
    b_up = {b: pltpu.roll(b_rows[b], M_HEADS, 0) for b in seqs}
    rmax = {b: gr[b] - b_up[b] for b in seqs}
    shift = 1
    while shift < L:
        for b in seqs:
            rmax[b] = jnp.maximum(rmax[b], jnp.where(lane8 >= shift, pltpu.roll(rmax[b], shift, 1), -jnp.inf))
        shift *= 2
    pieces = {}
    for b in seqs:
        xs = jnp.concatenate([b_rows[b], gr[b], rmax[b] + b_up[b]], axis=0)
        hi = xs.astype(BF16)
        r1 = xs - hi.astype(F32)
        mid = r1.astype(BF16)
        pieces[b] = (hi, mid, (r1 - mid.astype(F32)).astype(BF16))
    sel_r = lax.broadcasted_iota(jnp.int32, (3 * 2 * M_HEADS, 3 * LANES), 0)
    sel_c = lax.broadcasted_iota(jnp.int32, (3 * 2 * M_HEADS, 3 * LANES), 1)

    b_c, li_c, b_last, m_prev, caug, m_t, d_mat, inter_w = {}, {}, {}, {}, {}, {}, {}, {}
    for c in chains:
        b, h = c
        want = jnp.where(sel_c < LANES, M_HEADS + h, jnp.where(sel_c < 2 * LANES, 2 * M_HEADS + h, 4 * M_HEADS + h))
        sel = (sel_r == want).astype(BF16)
        rep = _dot_tn(pieces[b][0], sel) + _dot_tn(pieces[b][1], sel) + _dot_tn(pieces[b][2], sel)
        b_c[c], li_c[c], row_max = rep[:, :LANES], rep[:, LANES:2 * LANES], rep[:, 2 * LANES:]
        b_r = b_rows[b][M_HEADS + h:M_HEADS + h + 1, :]
        li_r = gr[b][h:h + 1, :]
        b_last[c] = b_c[c][L - 1:L, :]
        m_prev[c] = mst_refs[b * M_HEADS + h][0:1, :]
        caug[c] = cst_refs[b * M_HEADS + h][...]
        log_d = b_c[c] - b_r + li_r
        inter = b_c[c] + m_prev[c]
        m_t[c] = jnp.maximum(inter, row_max)
        d_mat[c] = jnp.where(causal, jnp.exp(log_d - m_t[c]), 0.0)
        inter_w[c] = jnp.exp(inter - m_t[c])

    twice = lambda x: jnp.concatenate([x, x], axis=1)
    s = {c: _dot_nt(qb[c], kb[c]) * d_mat[c] for c in chains}
    qc = {c: _dot(qb[c], caug[c].astype(BF16)) for c in chains}
    nd = {c: _dot(s[c].astype(BF16), vaug[c]) + twice(inter_w[c]) * qc[c] for c in chains}

    for c in chains:
        b, h = c
        g_c = b_last[c] - b_c[c] + li_c[c]
        m_loc = jnp.max(g_c, axis=0, keepdims=True)
        kw = (jnp.exp(g_c - m_loc) * kf[c]).astype(BF16)
        c_new = _dot_tn(kw, vaug[c])
        m_new = jnp.maximum(b_last[c] + m_prev[c], m_loc)
        a = jnp.exp(b_last[c] + m_prev[c] - m_new)
        bb = jnp.exp(m_loc - m_new)
        cst_refs[b * M_HEADS + h][...] = twice(a) * caug[c] + twice(bb) * c_new
        mst_refs[b * M_HEADS + h][...] = jnp.broadcast_to(m_new, (SUBLANES, LANES))

    mean_mat = jnp.full((M_HEAD_DIM, M_HEAD_DIM), 1.0 / M_HEAD_DIM, BF16)
    for c in chains:
        b, h = c
        num, den = nd[c][:, :M_HEAD_DIM], nd[c][:, M_HEAD_DIM:]
        hval = num / jnp.maximum(jnp.abs(den), jnp.exp(-m_t[c]))
        hn = hval * lax.rsqrt(_dot((hval * hval).astype(BF16), mean_mat) + EPS) * ng_ref[:, cols(h)]
        hn = hn + sk_ref[:, cols(h)] * ch[c]
        z = uvz_all[b, :, cols(h, 2 * M_WIDTH)]
        ym_all[b, :, cols(h)] = jax.nn.sigmoid(z) * hn


def _mlstm(uvz, grow, cw, cb, wq, wk, ng, sk, B, S):
    nc = S // CHUNK
    full = lambda a: pl.BlockSpec(a.shape, lambda c: (0,) * a.ndim)
    gate_specs = [pl.BlockSpec((2 * M_HEADS, CHUNK), lambda c, b=b: (0, b * nc + c)) for b in range(B)]
    ym = pl.pallas_call(
        _mlstm_kernel,
        grid=(nc,),
        in_specs=[pl.BlockSpec((B, CHUNK, 3 * M_WIDTH), lambda c: (0, c, 0))] + gate_specs
                 + [full(cw), full(cb), full(wq), full(wk), full(ng), full(sk)],
        out_specs=pl.BlockSpec((B, CHUNK, M_WIDTH), lambda c: (0, c, 0)),
        out_shape=jax.ShapeDtypeStruct((B, S, M_WIDTH), F32),
        scratch_shapes=([pltpu.VMEM((SUBLANES + CHUNK, M_WIDTH), F32)] * B
                        + [pltpu.VMEM((M_HEAD_DIM, 2 * M_HEAD_DIM), F32)] * (B * M_HEADS)
                        + [pltpu.VMEM((SUBLANES, LANES), F32)] * (B * M_HEADS)),
        compiler_params=_cparams(("arbitrary",)),
        name="mlstm",
    )(uvz.reshape(B, S, 3 * M_WIDTH), *([grow] * B), cw, cb, wq, wk, ng, sk)
    return ym.reshape(B * S, M_WIDTH)


ATT_BLK = DIL_PATTERNS[0][0] // DIL_PATTERNS[0][1]
ATT_SPAN = DIL_PATTERNS[-1][0]


ATT_R = DIL_PATTERNS[1][1]
assert [d for _, d in DIL_PATTERNS] == [1, ATT_R, ATT_R * ATT_R]
ATT_UNROLL = 8


def _dilattn_kernel(q_ref, kp_ref, kc_ref, vp_ref, vc_ref, bias_ref, o_ref,
                    q4_ref, k4_ref, v4_ref, k1_ref, v1_ref, a1_ref, a4_ref):
    blk, span, R = ATT_BLK, ATT_SPAN, ATT_R
    sub = span // R
    n = pl.program_id(1)
    scale = A_HEAD_DIM ** -0.5
    for r in range(R):
        res = pl.ds(r, sub, stride=R)
        q4_ref[r] = q_ref[res, :] * scale
        k4_ref[r, 0:sub] = kp_ref[res, :]
        k4_ref[r, sub:] = kc_ref[res, :]
        v4_ref[r, 0:sub] = vp_ref[res, :]
        v4_ref[r, sub:] = vc_ref[res, :]
    k1_ref[0:blk] = kp_ref[span - blk:, :]
    k1_ref[blk:] = kc_ref[...]
    v1_ref[0:blk] = vp_ref[span - blk:, :]
    v1_ref[blk:] = vc_ref[...]
    first_q =lax.broadcasted_iota(jnp.int32, (blk, LANES), 1) < A_HEAD_DIM

    def attend(q, kk, vv, bias_sel):
        kb = kk.astype(BF16)
        vb = jnp.concatenate([vv, jnp.ones(vv.shape, F32)], axis=1).astype(BF16)
        ms, ls, os_ = [], [], []
        for s_ in range(2):
            qh = jnp.where(first_q, q, 0.0) if s_ == 0 else jnp.where(first_q, 0.0, q)
            s = _dot_nt(qh.astype(BF16), kb) + bias_sel(s_)
            m = jnp.broadcast_to(jnp.max(s, axis=1, keepdims=True), (blk, LANES))
            e = jnp.exp(s - jnp.concatenate([m, m], axis=1))
            ol = _dot(e.astype(BF16), vb)
            ms.append(m)
            os_.append(ol[:, :LANES])
            ls.append(ol[:, LANES:])
        return (jnp.where(first_q, ms[0], ms[1]), jnp.where(first_q, ls[0], ls[1]),
                jnp.where(first_q, os_[0], os_[1]))

    def unit_d1(m, carry):
        base = pl.multiple_of(m * blk, blk)
        first = jnp.logical_and(n == 0, m == 0).astype(jnp.int32)
        m_c, l_c, o_c = attend(q_ref[pl.ds(base, blk), :] * scale, k1_ref[pl.ds(base, 2 * blk), :],
                               v1_ref[pl.ds(base, 2 * blk), :], lambda s_: bias_ref[0, first, s_])
        a1_ref[0, pl.ds(base, blk), :] = m_c
        a1_ref[1, pl.ds(base, blk), :] = l_c
        a1_ref[2, pl.ds(base, blk), :] = o_c
        return carry

    def unit_d4(u, carry):
        m4, r = u >> (R.bit_length() - 1), u & (R - 1)
        off = pl.multiple_of(m4 * blk, blk)
        first = jnp.logical_and(n == 0, m4 == 0).astype(jnp.int32)
        keys = pl.ds(sub - blk + off, 2 * blk)
        m_c, l_c, o_c = attend(q4_ref[r, pl.ds(off, blk), :], k4_ref[r, keys, :], v4_ref[r, keys, :],
                               lambda s_: bias_ref[1, first, s_])
        a4_ref[0, r, pl.ds(off, blk), :] = m_c
        a4_ref[1, r, pl.ds(off, blk), :] = l_c
        a4_ref[2, r, pl.ds(off, blk), :] = o_c
        return carry

    def unit_d16(u, carry):
        r, rp = u & (R - 1), u >> (R.bit_length() - 1)
        first = (n == 0).astype(jnp.int32)
        own, cur = pl.ds(rp, blk, stride=R), pl.ds(sub + rp, blk, stride=R)
        kk = jnp.concatenate([k4_ref[r, own, :], k4_ref[r, cur, :]], axis=0)
        vv = jnp.concatenate([v4_ref[r, own, :], v4_ref[r, cur, :]], axis=0)
        m_c, l_c, o_c = attend(q4_ref[r, own, :], kk, vv, lambda s_: bias_ref[2, first, s_])
        m_o = a4_ref[0, r, own, :]
        m_n = jnp.maximum(m_o, m_c)
        a_o, a_c = jnp.exp(m_o - m_n), jnp.exp(m_c - m_n)
        a4_ref[0, r, own, :] = m_n
        a4_ref[1, r, own, :] = a_o * a4_ref[1, r, own, :] + a_c * l_c
        a4_ref[2, r, own, :] = a_o * a4_ref[2, r, own, :] + a_c * o_c
        return carry

    units = span // blk
    lax.fori_loop(0, units, unit_d1, 0, unroll=ATT_UNROLL)
    lax.fori_loop(0, units, unit_d4, 0, unroll=ATT_UNROLL)
    lax.fori_loop(0, units, unit_d16, 0, unroll=ATT_UNROLL)

    for r in range(R):
        res = pl.ds(r, sub, stride=R)
        m1, m4 = a1_ref[0, res, :], a4_ref[0, r]
        m_n = jnp.maximum(m1, m4)
        e1, e4 = jnp.exp(m1 - m_n), jnp.exp(m4 - m_n)
        o_ref[res, :] = ((e1 * a1_ref[2, res, :] + e4 * a4_ref[2, r])
                         / (e1 * a1_ref[1, res, :] + e4 * a4_ref[1, r]))


def _dilattn(q, k, v, bias, B, S):
    span, blk, R = ATT_SPAN, ATT_BLK, ATT_R
    ns = S // span
    T = B * S
    cur = pl.BlockSpec((span, LANES), lambda b, n, hp: (b * ns + n, hp))
    prev = pl.BlockSpec((span, LANES), lambda b, n, hp: (b * ns + jnp.maximum(n - 1, 0), hp))
    return pl.pallas_call(
        _dilattn_kernel,
        grid=(B, ns, A_HEADS // 2),
        in_specs=[cur, prev, cur, prev, cur,
                  pl.BlockSpec((len(DIL_PATTERNS), 2, 2, blk, 2 * blk), lambda b, n, hp: (0, 0, hp, 0, 0))],
        out_specs=cur,
        out_shape=jax.ShapeDtypeStruct((T, A_WIDTH), F32),
        scratch_shapes=[pltpu.VMEM((R, span // R, LANES), F32),
                        pltpu.VMEM((R, 2 * span // R, LANES), F32),
                        pltpu.VMEM((R, 2 * span // R, LANES), F32),
                        pltpu.VMEM((blk + span, LANES), F32),
                        pltpu.VMEM((blk + span, LANES), F32),
                        pltpu.VMEM((3, span, LANES), F32),
                        pltpu.VMEM((3, R, span // R, LANES), F32)],
        compiler_params=_cparams(("parallel", "parallel", "parallel")),
        name="dilattn",
    )(q, k, k, v, v, bias)


def _attn_bias_kernel(bkt_ref, rb_ref, out_ref):
    h = pl.program_id(1)
    bkt = bkt_ref[...]
    acc = jnp.full(bkt.shape, -jnp.inf, F32)
    for kk in range(N_BUCKETS):
        acc = jnp.where(bkt == kk, rb_ref[kk, h], acc)
    col = lax.broadcasted_iota(jnp.int32, bkt.shape, 1)
    out_ref[0] = acc
    out_ref[1] = jnp.where(col < ATT_BLK, -jnp.inf, acc)


def _attn_bias(rel_bias):
    blk = ATT_BLK
    i = jnp.arange(blk)[:, None]
    j = jnp.arange(2 * blk)[None, :]
    steps = i + blk - j
    band = (steps >= 0) & (steps <= blk)
    bkt = jnp.stack([jnp.where(band, _t5_bucket(jnp.maximum(steps, 0) * d), -1) for _, d in DIL_PATTERNS])
    P = len(DIL_PATTERNS)
    return pl.pallas_call(
        _attn_bias_kernel,
        grid=(P, A_HEADS),
        in_specs=[pl.BlockSpec((None, blk, 2 * blk), lambda p, h: (p, 0, 0)),
                  pl.BlockSpec(memory_space=pltpu.SMEM)],
        out_specs=pl.BlockSpec((None, 2, None, blk, 2 * blk), lambda p, h: (p, 0, h, 0, 0)),
        out_shape=jax.ShapeDtypeStruct((P, 2, A_HEADS, blk, 2 * blk), F32),
        compiler_params=_cparams(("parallel", "parallel")),
        name="attn_bias",
    )(bkt.astype(jnp.int32), rel_bias)


def _t5_bucket(dist):
    max_exact = N_BUCKETS // 2
    nf = jnp.maximum(dist, 1).astype(F32)
    large = max_exact + (jnp.log(nf / max_exact) / np.log(MAX_DISTANCE / max_exact)
                         * (N_BUCKETS - max_exact)).astype(jnp.int32)
    large = jnp.minimum(large, N_BUCKETS - 1)
    return jnp.where(dist < max_exact, dist, large)


def _outproj_kernel(x_ref, ym_ref, ya_ref, wo_ref, g2_ref, x2_ref, h2t_ref):
    y = (_dot(ym_ref[...].astype(BF16), wo_ref[0:M_WIDTH, :])
         + _dot(ya_ref[...].astype(BF16), wo_ref[M_WIDTH:, :]))
    x2 = x_ref[...] + y
    x2_ref[...] = x2
    h2 = x2 * lax.rsqrt(jnp.mean(x2 * x2, axis=-1, keepdims=True) + EPS) * g2_ref[...]
    h2t_ref[...] = jnp.transpose(h2).astype(BF16)


def _outproj(x2d, ym, ya, wo, g2, tm):
    T, D = x2d.shape
    half = pl.BlockSpec((tm, M_WIDTH), lambda i: (i, 0))
    full = lambda a: pl.BlockSpec(a.shape, lambda i: (0,) * a.ndim)
    return pl.pallas_call(
        _outproj_kernel,
        grid=(T // tm,),
        in_specs=[pl.BlockSpec((tm, D), lambda i: (i, 0)), half, half, full(wo), full(g2)],
        out_specs=[pl.BlockSpec((tm, D), lambda i: (i, 0)), pl.BlockSpec((D, tm), lambda i: (0, i))],
        out_shape=[jax.ShapeDtypeStruct((T, D), F32), jax.ShapeDtypeStruct((D, T), BF16)],
        compiler_params=_cparams(("parallel",)),
        name="outproj",
    )(x2d, ym, ya, wo, g2)


def _peer_keys_kernel(keys_ref, wqt_ref, out_ref):
    out_ref[...] = jnp.dot(keys_ref[...], wqt_ref[...], preferred_element_type=F32,
                           precision=HI).astype(out_ref.dtype)


def _peer_keys(keys, wqt):
    _, H, K, C = keys.shape
    D = wqt.shape[-1]
    return pl.pallas_call(
        _peer_keys_kernel,
        grid=(2, H),
        in_specs=[pl.BlockSpec((None, None, K, C), lambda a, h: (a, h, 0, 0)),
                  pl.BlockSpec((None, None, C, D), lambda a, h: (a, h, 0, 0))],
        out_specs=pl.BlockSpec((None, None, K, D), lambda a, h: (a, h, 0, 0)),
        out_shape=jax.ShapeDtypeStruct((2, H, K, D), BF16),
        compiler_params=_cparams(("parallel", "parallel")),
        name="peer_keys",
    )(keys, wqt)


def _oddeven_merge_sort_pairs(n):
    pairs = []

    def merge(lo, hi, r):
        step = r * 2
        if step < hi - lo:
            merge(lo, hi, step)
            merge(lo + r, hi, step)
            for i in range(lo + r, hi - r, step):
                pairs.append((i, i + r))
        else:
            pairs.append((lo, lo + r))

    def sort(lo, hi):
        if hi - lo >= 1:
            mid = lo + (hi - lo) // 2
            sort(lo, mid)
            sort(mid + 1, hi)
            merge(lo, hi, 1)

    sort(0, n - 1)
    return pairs


_SORT16 = _oddeven_merge_sort_pairs(PEER_TOPK)


def _sort_desc(w):
    w = list(w)
    for a, b in _SORT16:
        hi, lo = jnp.maximum(w[a], w[b]), jnp.minimum(w[a], w[b])
        w[a], w[b] = hi, lo
    return w


def _merge_top(a, b):
    n = len(a)
    w = [jnp.maximum(a[i], b[n - 1 - i]) for i in range(n)]
    half = n // 2
    while half >= 1:
        for start in range(0, n, 2 * half):
            for i in range(start, start + half):
                hi, lo = jnp.maximum(w[i], w[i + half]), jnp.minimum(w[i], w[i + half])
                w[i], w[i + half] = hi, lo
        half //= 2
    return w


def _peer_topk_kernel(wk_ref, h2t_ref, ub_ref, nb_ref, vb_ref, rk_ref, sc_ref):
    H, K = PEER_HEADS, N_KEYS
    tb = h2t_ref.shape[1]
    h2t = h2t_ref[...]
    for a in range(2):
        for h in range(H):
            sc_ref[a, h] = _dot(wk_ref[a, h], h2t)

    sub = lax.broadcasted_iota(jnp.int32, (SUBLANES, LANES), 0)
    zeros = jnp.zeros((SUBLANES, LANES), F32)
    for lt in range(tb // LANES):
        ls = slice(lt * LANES, (lt + 1) * LANES)
        packed = []
        for a in range(2):
            acc = [None] * PEER_TOPK
            for h in range(H):
                w = [sc_ref[a, h, v * SUBLANES:(v + 1) * SUBLANES, ls] for v in range(K // SUBLANES)]
                w = _sort_desc(w)
                for shift in (4, 2, 1):
                    w = _merge_top(w, [pltpu.roll(x, shift, 0) for x in w])
                for i in range(PEER_TOPK):
                    acc[i] = w[i] if h == 0 else jnp.where(sub == h, w[i], acc[i])
            packed.append(acc)
        v1, v2 = packed
        rows = [[v1[a] + v2[b] for b in range(PEER_TOPK // (a + 1))] for a in range(PEER_TOPK)]
        neg = jnp.full((SUBLANES, LANES), -jnp.inf, F32)
        l0 = rows[0]
        l1 = _sort_desc(rows[1] + rows[2] + rows[4])
        l2 = _sort_desc(rows[3] + rows[5] + rows[6] + rows[7] + [rows[a][0] for a in range(8, 14)])
        l3 = [rows[14][0], rows[15][0]]
        l3 = [jnp.maximum(l3[0], l3[1]), jnp.minimum(l3[0], l3[1])] + [neg] * (PEER_TOPK - 2)
        vc = _merge_top(_merge_top(l0, l1), _merge_top(l2, l3))
        top, tau = vc[0], vc[PEER_TOPK - 1]
        zsum = zeros
        for i in range(PEER_TOPK):
            zsum = zsum + jnp.exp(vc[i] - top)
        inv_z = 1.0 / zsum
        n_sel = []
        for a in range(PEER_TOPK):
            cnt = zeros
            for cand in rows[a]:
                cnt = cnt + jnp.where(cand >= tau, 1.0, 0.0)
            n_sel.append(cnt)
        m1, m2 = v1[0], v2[0]
        for h in range(H):
            bc = lambda x: jnp.broadcast_to(x[h:h + 1, :], (SUBLANES, LANES))
            w1 = [bc(x) for x in v1]
            w2 = [bc(x) for x in v2]
            na = [bc(x) for x in n_sel]
            m1h, m2h, izh = bc(m1), bc(m2), bc(inv_z)
            ub, nb, vb, rk = [], [], [], []
            for v in range(K // SUBLANES):
                s1 = sc_ref[0, h, v * SUBLANES:(v + 1) * SUBLANES, ls]
                s2 = sc_ref[1, h, v * SUBLANES:(v + 1) * SUBLANES, ls]
                nbv, rkv = zeros, zeros
                for a in range(PEER_TOPK):
                    nbv = jnp.where(s1 == w1[a], na[a], nbv)
                    rkv = jnp.where(w2[a] > s2, float(a + 1), rkv)
                ub.append(jnp.exp(s1 - m1h) * izh * 0.5)
                vb.append(jnp.exp(s2 - m2h))
                nb.append(nbv)
                rk.append(rkv)
            ub_ref[h, :, ls] = jnp.concatenate(ub, axis=0)
            nb_ref[h, :, ls] = jnp.concatenate(nb, axis=0)
            vb_ref[h, :, ls] = jnp.concatenate(vb, axis=0).astype(BF16)
            rk_ref[h, :, ls] = jnp.concatenate(rk, axis=0).astype(BF16)


def _peer_topk(wk, h2t, tb):
    D, T = h2t.shape
    H, K = PEER_HEADS, N_KEYS
    spec = pl.BlockSpec((H, K, tb), lambda i: (0, 0, i))
    return pl.pallas_call(
        _peer_topk_kernel,
        grid=(T // tb,),
        in_specs=[pl.BlockSpec(wk.shape, lambda i: (0, 0, 0, 0)), pl.BlockSpec((D, tb), lambda i: (0, i))],
        out_specs=[spec] * 4,
        out_shape=[jax.ShapeDtypeStruct((H, K, T), F32)] * 2 + [jax.ShapeDtypeStruct((H, K, T), BF16)] * 2,
        scratch_shapes=[pltpu.VMEM((2, H, K, tb), F32)],
        compiler_params=_cparams(("parallel",)),
        name="peer_topk",
    )(wk, h2t)


def _peer_main_kernel(h2t_ref, ue_ref, vet_ref, ub_ref, nb_ref, vb_ref, rk_ref, x2_ref, out_ref,
                      acc_ref, a_ref):
    c = pl.program_id(1)
    n_i = ue_ref.shape[0] // N_KEYS

    @pl.when(c == 0)
    def _():
        acc_ref[...] = jnp.zeros_like(acc_ref)

    h2t = h2t_ref[...]
    for ii in range(n_i):
        rs = slice(ii * N_KEYS, (ii + 1) * N_KEYS)
        pre = _dot(ue_ref[rs, :], h2t)
        act = (pre * (lax.erf(pre * (2.0 ** -0.5)) + 1.0)).astype(BF16)
        gate = jnp.zeros(pre.shape, BF16)
        for h in range(PEER_HEADS):
            row = lambda ref: jnp.broadcast_to(ref[h, ii:ii + 1, :], pre.shape).astype(BF16)
            gate = gate + jnp.where(rk_ref[h] < row(nb_ref), row(ub_ref) * vb_ref[h], 0)
        a_ref[rs, :] = act * gate
    acc_ref[...] += _dot(vet_ref[...], a_ref[...])

    @pl.when(c == pl.num_programs(1) - 1)
    def _():
        out_ref[...] = x2_ref[...] + jnp.transpose(acc_ref[...])


def _peer_main(h2t, ue, vet, ub, nb, vb, rk, x2, tb, ec):
    D, T = h2t.shape
    E = ue.shape[0]
    H, K = PEER_HEADS, N_KEYS
    row = pl.BlockSpec((H, ec // K, tb), lambda t, c: (0, c, t))
    col = pl.BlockSpec((H, K, tb), lambda t, c: (0, 0, t))
    return pl.pallas_call(
        _peer_main_kernel,
        grid=(T // tb, E // ec),
        in_specs=[pl.BlockSpec((D, tb), lambda t, c: (0, t)),
                  pl.BlockSpec((ec, D), lambda t, c: (c, 0)),
                  pl.BlockSpec((D, ec), lambda t, c: (0, c)),
                  row, row, col, col,
                  pl.BlockSpec((tb, D), lambda t, c: (t, 0))],
        out_specs=pl.BlockSpec((tb, D), lambda t, c: (t, 0)),
        out_shape=jax.ShapeDtypeStruct((T, D), F32),
        scratch_shapes=[pltpu.VMEM((D, tb), F32), pltpu.VMEM((ec, tb), BF16)],
        compiler_params=_cparams(("parallel", "arbitrary")),
        name="peer_main",
    )(h2t, ue, vet, ub, nb, vb, rk, x2)


def _block_rows(T, want):
    tm = min(T, want)
    assert T % tm == 0
    return tm


def _layer(x2d, B, S, norm1_g, w_in, conv_w, conv_b, wq_m, wk_m, ig_b, fg_b, mh_norm_g, skip_m,
           qn_g, kn_g, rel_bias, w_out, norm2_g, w_query, sub_keys1, sub_keys2, expert_u, expert_v):
    T, D = x2d.shape
    o_vm, o_z, o_i, o_f, o_q = M_WIDTH, 2 * M_WIDTH, 3 * M_WIDTH, 3 * M_WIDTH + M_HEADS, 3 * M_WIDTH + 2 * M_HEADS
    wm = w_in[:, :o_i].astype(BF16)
    wa = w_in[:, o_q:].astype(BF16)
    wgt = jnp.transpose(w_in[:, o_i:o_q]).astype(BF16)
    gb = jnp.concatenate([ig_b, fg_b])[:, None]
    seg = np.arange(A_WIDTH) // A_HEAD_DIM
    bd = jnp.asarray(seg[:, None] == seg[None, :], BF16)
    qkg = jnp.stack([qn_g.reshape(-1), kn_g.reshape(-1)])
    tm = _block_rows(T, 512)
    uvz, qn, kn, va, grow = _inproj(x2d, norm1_g[None, :], wm, wa, wgt, gb, bd, qkg, tm)

    ym = _mlstm(uvz, grow, conv_w, conv_b[None, :], wq_m.astype(BF16), wk_m.astype(BF16),
                mh_norm_g.reshape(1, -1), skip_m.reshape(1, -1), B, S)

    ya = _dilattn(qn, kn, va, _attn_bias(rel_bias), B, S)

    x2, h2t = _outproj(x2d, ym, ya, w_out.astype(BF16), norm2_g[None, :], tm)

    keys = jnp.stack([sub_keys1, sub_keys2])
    wqt = jnp.transpose(w_query.reshape(D, PEER_HEADS, 2, PEER_QDIM // 2), (2, 1, 3, 0))
    wk = _peer_keys(keys, wqt)
    ub, nb, vb, rk = _peer_topk(wk, h2t, _block_rows(T, 256))
    ue = expert_u.astype(BF16)
    vet = jnp.transpose(expert_v).astype(BF16)
    return _peer_main(h2t, ue, vet, ub, nb, vb, rk, x2, _block_rows(T, 512), 2048)


def kernel(x, norm1_g, w_in, conv_w, conv_b, wq_m, wk_m, ig_b, fg_b, mh_norm_g, skip_m, qn_g, kn_g,
           rel_bias, w_out, norm2_g, w_query, sub_keys1, sub_keys2, expert_u, expert_v):
    B, S, D = x.shape
    assert S % DIL_PATTERNS[-1][0] == 0 and S % CHUNK == 0
    x2d = x.reshape(B * S, D)
    for l in range(norm1_g.shape[0]):
        x2d = _layer(x2d, B, S, norm1_g[l], w_in[l], conv_w[l], conv_b[l], wq_m[l], wk_m[l], ig_b[l],
                     fg_b[l], mh_norm_g[l], skip_m[l], qn_g[l], kn_g[l], rel_bias, w_out[l], norm2_g[l],
                     w_query[l], sub_keys1[l], sub_keys2[l], expert_u[l], expert_v[l])
    return x2d.reshape(B, S, D)
```

```python
import functools
import math

import numpy as np
import jax
import jax.numpy as jnp
from jax import lax
from jax.experimental import pallas as pl
from jax.experimental.pallas import tpu as pltpu

EPS = 1e-6
M_HEADS = 4
M_HEAD_DIM = 128
M_WIDTH = M_HEADS * M_HEAD_DIM
CONV_W = 4
CHUNK = 128
A_HEADS = 8
A_HEAD_DIM = 64
A_WIDTH = A_HEADS * A_HEAD_DIM
DIL_PATTERNS = ((128, 1), (512, 4), (2048, 16))
N_BUCKETS = 32
MAX_DISTANCE = 2048
PEER_HEADS = 8
N_KEYS = 128
PEER_QDIM = 256
PEER_TOPK = 16

LANES = 128
SUBLANES = 8
VMEM_LIMIT = 56 * 1024 * 1024

BF16 = jnp.bfloat16
F32 = jnp.float32
HI = lax.Precision.HIGHEST


def _cparams(sem):
    return pltpu.CompilerParams(dimension_semantics=sem, vmem_limit_bytes=VMEM_LIMIT)


def _dot(a, b):
    return jnp.dot(a, b, preferred_element_type=F32)


def _dot_nt(a, b):
    return lax.dot_general(a, b, (((1,), (1,)), ((), ())), preferred_element_type=F32)


def _dot_tn(a, b):
    return lax.dot_general(a, b, (((0,), (0,)), ((), ())), preferred_element_type=F32)


def _inproj_kernel(x_ref, g1_ref, wm_ref, wa_ref, wgt_ref, gb_ref, bd_ref, qkg_ref,
                   uvz_ref, q_ref, k_ref, v_ref, grow_ref):
    x = x_ref[...]
    h = x * lax.rsqrt(jnp.mean(x * x, axis=-1, keepdims=True) + EPS) * g1_ref[...]
    hb = h.astype(BF16)
    uvz_ref[...] = _dot(hb, wm_ref[...])
    qkv = _dot(hb, wa_ref[...])
    bd = bd_ref[...]

    def head_norm(t, g):
        ms = _dot((t * t).astype(BF16), bd) * (1.0 / A_HEAD_DIM)
        return t * lax.rsqrt(ms + EPS) * g

    q_ref[...] = head_norm(qkv[:, :A_WIDTH], qkg_ref[0:1, :])
    k_ref[...] = head_norm(qkv[:, A_WIDTH:2 * A_WIDTH], qkg_ref[1:2, :])
    v_ref[...] = qkv[:, 2 * A_WIDTH:]
    gr = _dot_nt(wgt_ref[...], hb) + gb_ref[...]
    lf = jnp.minimum(gr, 0.0) - jnp.log1p(jnp.exp(-jnp.abs(gr)))
    row = lax.broadcasted_iota(jnp.int32, gr.shape, 0)
    grow_ref[...] = jnp.where(row < M_HEADS, gr, lf)


def _inproj(x2d, g1, wm, wa, wgt, gb, bd, qkg, tm):
    T, D = x2d.shape
    full = lambda a: pl.BlockSpec(a.shape, lambda i: (0,) * a.ndim)
    return pl.pallas_call(
        _inproj_kernel,
        grid=(T // tm,),
        in_specs=[pl.BlockSpec((tm, D), lambda i: (i, 0)), full(g1), full(wm), full(wa), full(wgt),
                  full(gb), full(bd), full(qkg)],
        out_specs=[pl.BlockSpec((tm, 3 * M_WIDTH), lambda i: (i, 0)),
                   pl.BlockSpec((tm, A_WIDTH), lambda i: (i, 0)),
                   pl.BlockSpec((tm, A_WIDTH), lambda i: (i, 0)),
                   pl.BlockSpec((tm, A_WIDTH), lambda i: (i, 0)),
                   pl.BlockSpec((2 * M_HEADS, tm), lambda i: (0, i))],
        out_shape=[jax.ShapeDtypeStruct((T, 3 * M_WIDTH), F32),
                   jax.ShapeDtypeStruct((T, A_WIDTH), F32),
                   jax.ShapeDtypeStruct((T, A_WIDTH), F32),
                   jax.ShapeDtypeStruct((T, A_WIDTH), F32),
                   jax.ShapeDtypeStruct((2 * M_HEADS, T), F32)],
        compiler_params=_cparams(("parallel",)),
        name="inproj",
    )(x2d, g1, wm, wa, wgt, gb, bd, qkg)


def _mlstm_kernel(*refs):
    nb = (len(refs) - 8) // (2 + 2 * M_HEADS)
    uvz_all, grow_refs = refs[0], refs[1:1 + nb]
    cw_ref, cb_ref, wq_ref, wk_ref, ng_ref, sk_ref, ym_all = refs[1 + nb:8 + nb]
    ubuf_refs = refs[8 + nb:8 + 2 * nb]
    cst_refs = refs[8 + 2 * nb:8 + 2 * nb + nb * M_HEADS]
    mst_refs = refs[8 + 2 * nb + nb * M_HEADS:]
    L = CHUNK

    @pl.when(pl.program_id(0) == 0)
    def _():
        for r in ubuf_refs:
            r[0:SUBLANES, :] = jnp.zeros((SUBLANES, M_WIDTH), F32)
        for r in cst_refs + mst_refs:
            r[...] = jnp.zeros_like(r)

    rr = lax.broadcasted_iota(jnp.int32, (L, L), 0)
    cc = lax.broadcasted_iota(jnp.int32, (L, L), 1)
    causal = cc <= rr
    lane8 = lax.broadcasted_iota(jnp.int32, (2 * M_HEADS, L), 1)
    seqs = range(nb)
    chains = [(b, h) for b in seqs for h in range(M_HEADS)]
    cols = lambda h, off=0: slice(off + h * M_HEAD_DIM, off + (h + 1) * M_HEAD_DIM)

    cact, gr, b_rows = {}, {}, {}
    for b in seqs:
        uvz_ref, ubuf_ref = uvz_all.at[b], ubuf_refs[b]
        u = uvz_ref[:, 0:M_WIDTH]
        ubuf_ref[SUBLANES:SUBLANES + L, :] = u
        conv = cb_ref[...] + jnp.zeros((L, M_WIDTH), F32)
        for w in range(CONV_W):
            conv = conv + ubuf_ref[pl.ds(SUBLANES - (CONV_W - 1) + w, L), :] * cw_ref[w:w + 1, :]
        ubuf_ref[0:SUBLANES, :] = u[L - SUBLANES:, :]
        cact[b] = conv * jax.nn.sigmoid(conv)
        gr[b] = grow_refs[b][...]
    ch, qb, kb, kf, vaug = {}, {}, {}, {}, {}
    for c in chains:
        b, h = c
        ch[c] = cact[b][:, cols(h)]
        chb = ch[c].astype(BF16)
        qb[c] = _dot(chb, wq_ref[h]).astype(BF16)
        kf[c] = _dot(chb, wk_ref[h]) * (M_HEAD_DIM ** -0.5)
        kb[c] = kf[c].astype(BF16)
        v = uvz_all[b, :, cols(h, M_WIDTH)]
        vaug[c] = jnp.concatenate([v, jnp.ones((L, M_HEAD_DIM), F32)], axis=1).astype(BF16)

    for b in seqs:
        b_rows[b] = gr[b]
    shift = 1
    while shift < L:
        for b in seqs:
            b_rows[b] = b_rows[b] + jnp.where(lane8 >= shift, pltpu.roll(b_rows[b], shift, 1), 0.0)
        shift *= 2
    b_up = {b: pltpu.roll(b_rows[b], M_HEADS, 0) for b in seqs}
    rmax = {b: gr[b] - b_up[b] for b in seqs}
    shift = 1
    while shift < L:
        for b in seqs:
            rmax[b] = jnp.maximum(rmax[b], jnp.where(lane8 >= shift, pltpu.roll(rmax[b], shift, 1), -jnp.inf))
        shift *= 2
    pieces = {}
    for b in seqs:
        xs = jnp.concatenate([b_rows[b], gr[b], rmax[b] + b_up[b]], axis=0)
        hi = xs.astype(BF16)
        r1 = xs - hi.astype(F32)
        mid = r1.astype(BF16)
        pieces[b] = (hi, mid, (r1 - mid.astype(F32)).astype(BF16))
    sel_r = lax.broadcasted_iota(jnp.int32, (3 * 2 * M_HEADS, 3 * LANES), 0)
    sel_c = lax.broadcasted_iota(jnp.int32, (3 * 2 * M_HEADS, 3 * LANES), 1)

    b_c, li_c, b_last, m_prev, caug, m_t, d_mat, inter_w = {}, {}, {}, {}, {}, {}, {}, {}
    for c in chains:
        b, h = c
        want = jnp.where(sel_c < LANES, M_HEADS + h, jnp.where(sel_c < 2 * LANES, 2 * M_HEADS + h, 4 * M_HEADS + h))
        sel = (sel_r == want).astype(BF16)
        rep = _dot_tn(pieces[b][0], sel) + _dot_tn(pieces[b][1], sel) + _dot_tn(pieces[b][2], sel)
        b_c[c], li_c[c], row_max = rep[:, :LANES], rep[:, LANES:2 * LANES], rep[:, 2 * LANES:]
        b_r = b_rows[b][M_HEADS + h:M_HEADS + h + 1, :]
        li_r = gr[b][h:h + 1, :]
        b_last[c] = b_c[c][L - 1:L, :]
        m_prev[c] = mst_refs[b * M_HEADS + h][0:1, :]
        caug[c] = cst_refs[b * M_HEADS + h][...]
        log_d = b_c[c] - b_r + li_r
        inter = b_c[c] + m_prev[c]
        m_t[c] = jnp.maximum(inter, row_max)
        d_mat[c] = jnp.where(causal, jnp.exp(log_d - m_t[c]), 0.0)
        inter_w[c] = jnp.exp(inter - m_t[c])

    twice = lambda x: jnp.concatenate([x, x], axis=1)
    s = {c: _dot_nt(qb[c], kb[c]) * d_mat[c] for c in chains}
    qc = {c: _dot(qb[c], caug[c].astype(BF16)) for c in chains}
    nd = {c: _dot(s[c].astype(BF16), vaug[c]) + twice(inter_w[c]) * qc[c] for c in chains}

    for c in chains:
        b, h = c
        g_c = b_last[c] - b_c[c] + li_c[c]
        m_loc = jnp.max(g_c, axis=0, keepdims=True)
        kw = (jnp.exp(g_c - m_loc) * kf[c]).astype(BF16)
        c_new = _dot_tn(kw, vaug[c])
        m_new = jnp.maximum(b_last[c] + m_prev[c], m_loc)
        a = jnp.exp(b_last[c] + m_prev[c] - m_new)
        bb = jnp.exp(m_loc - m_new)
        cst_refs[b * M_HEADS + h][...] = twice(a) * caug[c] + twice(bb) * c_new
        mst_refs[b * M_HEADS + h][...] = jnp.broadcast_to(m_new, (SUBLANES, LANES))

    mean_mat = jnp.full((M_HEAD_DIM, M_HEAD_DIM), 1.0 / M_HEAD_DIM, BF16)
    for c in chains:
        b, h = c
        num, den = nd[c][:, :M_HEAD_DIM], nd[c][:, M_HEAD_DIM:]
        hval = num / jnp.maximum(jnp.abs(den), jnp.exp(-m_t[c]))
        hn = hval * lax.rsqrt(_dot((hval * hval).astype(BF16), mean_mat) + EPS) * ng_ref[:, cols(h)]
        hn = hn + sk_ref[:, cols(h)] * ch[c]
        z = uvz_all[b, :, cols(h, 2 * M_WIDTH)]
        ym_all[b, :, cols(h)] = jax.nn.sigmoid(z) * hn


def _mlstm(uvz, grow, cw, cb, wq, wk, ng, sk, B, S):
    nc = S // CHUNK
    full = lambda a: pl.BlockSpec(a.shape, lambda c: (0,) * a.ndim)
    gate_specs = [pl.BlockSpec((2 * M_HEADS, CHUNK), lambda c, b=b: (0, b * nc + c)) for b in range(B)]
    ym = pl.pallas_call(
        _mlstm_kernel,
        grid=(nc,),
        in_specs=[pl.BlockSpec((B, CHUNK, 3 * M_WIDTH), lambda c: (0, c, 0))] + gate_specs
                 + [full(cw), full(cb), full(wq), full(wk), full(ng), full(sk)],
        out_specs=pl.BlockSpec((B, CHUNK, M_WIDTH), lambda c: (0, c, 0)),
        out_shape=jax.ShapeDtypeStruct((B, S, M_WIDTH), F32),
        scratch_shapes=([pltpu.VMEM((SUBLANES + CHUNK, M_WIDTH), F32)] * B
                        + [pltpu.VMEM((M_HEAD_DIM, 2 * M_HEAD_DIM), F32)] * (B * M_HEADS)
                        + [pltpu.VMEM((SUBLANES, LANES), F32)] * (B * M_HEADS)),
        compiler_params=_cparams(("arbitrary",)),
        name="mlstm",
    )(uvz.reshape(B, S, 3 * M_WIDTH), *([grow] * B), cw, cb, wq, wk, ng, sk)
    return ym.reshape(B * S, M_WIDTH)


ATT_BLK = DIL_PATTERNS[0][0] // DIL_PATTERNS[0][1]
ATT_SPAN = DIL_PATTERNS[-1][0]


ATT_R = DIL_PATTERNS[1][1]
assert [d for _, d in DIL_PATTERNS] == [1, ATT_R, ATT_R * ATT_R]
ATT_UNROLL = 16


def _dilattn_kernel(q_ref, kp_ref, kc_ref, vp_ref, vc_ref, bias_ref, o_ref,
                    q4_ref, k4_ref, v4_ref, k1_ref, v1_ref, a1_ref, a4_ref):
    blk, span, R = ATT_BLK, ATT_SPAN, ATT_R
    sub = span // R
    n = pl.program_id(1)
    scale = A_HEAD_DIM ** -0.5
    for r in range(R):
        res = pl.ds(r, sub, stride=R)
        q4_ref[r] = q_ref[res, :] * scale
        k4_ref[r, 0:sub] = kp_ref[res, :]
        k4_ref[r, sub:] = kc_ref[res, :]
        v4_ref[r, 0:sub] = vp_ref[res, :]
        v4_ref[r, sub:] = vc_ref[res, :]
    k1_ref[0:blk] = kp_ref[span - blk:, :]
    k1_ref[blk:] = kc_ref[...]
    v1_ref[0:blk] = vp_ref[span - blk:, :]
    v1_ref[blk:] = vc_ref[...]
    first_q =lax.broadcasted_iota(jnp.int32, (blk, LANES), 1) < A_HEAD_DIM

    def attend(q, kk, vv, bias_sel):
        kb = kk.astype(BF16)
        vb = jnp.concatenate([vv, jnp.ones(vv.shape, F32)], axis=1).astype(BF16)
        ms, ls, os_ = [], [], []
        for s_ in range(2):
            qh = jnp.where(first_q, q, 0.0) if s_ == 0 else jnp.where(first_q, 0.0, q)
            s = _dot_nt(qh.astype(BF16), kb) + bias_sel(s_)
            m = jnp.broadcast_to(jnp.max(s, axis=1, keepdims=True), (blk, LANES))
            e = jnp.exp(s - jnp.concatenate([m, m], axis=1))
            ol = _dot(e.astype(BF16), vb)
            ms.append(m)
            os_.append(ol[:, :LANES])
            ls.append(ol[:, LANES:])
        return (jnp.where(first_q, ms[0], ms[1]), jnp.where(first_q, ls[0], ls[1]),
                jnp.where(first_q, os_[0], os_[1]))

    def unit_d1(m, carry):
        base = pl.multiple_of(m * blk, blk)
        first = jnp.logical_and(n == 0, m == 0).astype(jnp.int32)
        m_c, l_c, o_c = attend(q_ref[pl.ds(base, blk), :] * scale, k1_ref[pl.ds(base, 2 * blk), :],
                               v1_ref[pl.ds(base, 2 * blk), :], lambda s_: bias_ref[0, first, s_])
        a1_ref[0, pl.ds(base, blk), :] = m_c
        a1_ref[1, pl.ds(base, blk), :] = l_c
        a1_ref[2, pl.ds(base, blk), :] = o_c
        return carry

    def unit_d4(u, carry):
        m4, r = u >> (R.bit_length() - 1), u & (R - 1)
        off = pl.multiple_of(m4 * blk, blk)
        first = jnp.logical_and(n == 0, m4 == 0).astype(jnp.int32)
        keys = pl.ds(sub - blk + off, 2 * blk)
        m_c, l_c, o_c = attend(q4_ref[r, pl.ds(off, blk), :], k4_ref[r, keys, :], v4_ref[r, keys, :],
                               lambda s_: bias_ref[1, first, s_])
        a4_ref[0, r, pl.ds(off, blk), :] = m_c
        a4_ref[1, r, pl.ds(off, blk), :] = l_c
        a4_ref[2, r, pl.ds(off, blk), :] = o_c
        return carry

    def unit_d16(u, carry):
        r, rp = u & (R - 1), u >> (R.bit_length() - 1)
        first = (n == 0).astype(jnp.int32)
        own, cur = pl.ds(rp, blk, stride=R), pl.ds(sub + rp, blk, stride=R)
        kk = jnp.concatenate([k4_ref[r, own, :], k4_ref[r, cur, :]], axis=0)
        vv = jnp.concatenate([v4_ref[r, own, :], v4_ref[r, cur, :]], axis=0)
        m_c, l_c, o_c = attend(q4_ref[r, own, :], kk, vv, lambda s_: bias_ref[2, first, s_])
        m_o = a4_ref[0, r, own, :]
        m_n = jnp.maximum(m_o, m_c)
        a_o, a_c = jnp.exp(m_o - m_n), jnp.exp(m_c - m_n)
        a4_ref[0, r, own, :] = m_n
        a4_ref[1, r, own, :] = a_o * a4_ref[1, r, own, :] + a_c * l_c
        a4_ref[2, r, own, :] = a_o * a4_ref[2, r, own, :] + a_c * o_c
        return carry

    units = span // blk
    lax.fori_loop(0, units, unit_d1, 0, unroll=ATT_UNROLL)
    lax.fori_loop(0, units, unit_d4, 0, unroll=ATT_UNROLL)
    lax.fori_loop(0, units, unit_d16, 0, unroll=ATT_UNROLL)

    for r in range(R):
        res = pl.ds(r, sub, stride=R)
        m1, m4 = a1_ref[0, res, :], a4_ref[0, r]
        m_n = jnp.maximum(m1, m4)
        e1, e4 = jnp.exp(m1 - m_n), jnp.exp(m4 - m_n)
        o_ref[res, :] = ((e1 * a1_ref[2, res, :] + e4 * a4_ref[2, r])
                         / (e1 * a1_ref[1, res, :] + e4 * a4_ref[1, r]))


def _dilattn(q, k, v, bias, B, S):
    span, blk, R = ATT_SPAN, ATT_BLK, ATT_R
    ns = S // span
    T = B * S
    cur = pl.BlockSpec((span, LANES), lambda b, n, hp: (b * ns + n, hp))
    prev = pl.BlockSpec((span, LANES), lambda b, n, hp: (b * ns + jnp.maximum(n - 1, 0), hp))
    return pl.pallas_call(
        _dilattn_kernel,
        grid=(B, ns, A_HEADS // 2),
        in_specs=[cur, prev, cur, prev, cur,
                  pl.BlockSpec((len(DIL_PATTERNS), 2, 2, blk, 2 * blk), lambda b, n, hp: (0, 0, hp, 0, 0))],
        out_specs=cur,
        out_shape=jax.ShapeDtypeStruct((T, A_WIDTH), F32),
        scratch_shapes=[pltpu.VMEM((R, span // R, LANES), F32),
                        pltpu.VMEM((R, 2 * span // R, LANES), F32),
                        pltpu.VMEM((R, 2 * span // R, LANES), F32),
                        pltpu.VMEM((blk + span, LANES), F32),
                        pltpu.VMEM((blk + span, LANES), F32),
                        pltpu.VMEM((3, span, LANES), F32),
                        pltpu.VMEM((3, R, span // R, LANES), F32)],
        compiler_params=_cparams(("parallel", "parallel", "parallel")),
        name="dilattn",
    )(q, k, k, v, v, bias)


def _attn_bias_kernel(bkt_ref, rb_ref, out_ref):
    h = pl.program_id(1)
    bkt = bkt_ref[...]
    acc = jnp.full(bkt.shape, -jnp.inf, F32)
    for kk in range(N_BUCKETS):
        acc = jnp.where(bkt == kk, rb_ref[kk, h], acc)
    col = lax.broadcasted_iota(jnp.int32, bkt.shape, 1)
    out_ref[0] = acc
    out_ref[1] = jnp.where(col < ATT_BLK, -jnp.inf, acc)


def _attn_bias(rel_bias):
    blk = ATT_BLK
    i = jnp.arange(blk)[:, None]
    j = jnp.arange(2 * blk)[None, :]
    steps = i + blk - j
    band = (steps >= 0) & (steps <= blk)
    bkt = jnp.stack([jnp.where(band, _t5_bucket(jnp.maximum(steps, 0) * d), -1) for _, d in DIL_PATTERNS])
    P = len(DIL_PATTERNS)
    return pl.pallas_call(
        _attn_bias_kernel,
        grid=(P, A_HEADS),
        in_specs=[pl.BlockSpec((None, blk, 2 * blk), lambda p, h: (p, 0, 0)),
                  pl.BlockSpec(memory_space=pltpu.SMEM)],
        out_specs=pl.BlockSpec((None, 2, None, blk, 2 * blk), lambda p, h: (p, 0, h, 0, 0)),
        out_shape=jax.ShapeDtypeStruct((P, 2, A_HEADS, blk, 2 * blk), F32),
        compiler_params=_cparams(("parallel", "parallel")),
        name="attn_bias",
    )(bkt.astype(jnp.int32), rel_bias)


def _t5_bucket(dist):
    max_exact = N_BUCKETS // 2
    nf = jnp.maximum(dist, 1).astype(F32)
    large = max_exact + (jnp.log(nf / max_exact) / np.log(MAX_DISTANCE / max_exact)
                         * (N_BUCKETS - max_exact)).astype(jnp.int32)
    large = jnp.minimum(large, N_BUCKETS - 1)
    return jnp.where(dist < max_exact, dist, large)


def _outproj_kernel(x_ref, ym_ref, ya_ref, wo_ref, g2_ref, x2_ref, h2t_ref):
    y = (_dot(ym_ref[...].astype(BF16), wo_ref[0:M_WIDTH, :])
         + _dot(ya_ref[...].astype(BF16), wo_ref[M_WIDTH:, :]))
    x2 = x_ref[...] + y
    x2_ref[...] = x2
    h2 = x2 * lax.rsqrt(jnp.mean(x2 * x2, axis=-1, keepdims=True) + EPS) * g2_ref[...]
    h2t_ref[...] = jnp.transpose(h2).astype(BF16)


def _outproj(x2d, ym, ya, wo, g2, tm):
    T, D = x2d.shape
    half = pl.BlockSpec((tm, M_WIDTH), lambda i: (i, 0))
    full = lambda a: pl.BlockSpec(a.shape, lambda i: (0,) * a.ndim)
    return pl.pallas_call(
        _outproj_kernel,
        grid=(T // tm,),
        in_specs=[pl.BlockSpec((tm, D), lambda i: (i, 0)), half, half, full(wo), full(g2)],
        out_specs=[pl.BlockSpec((tm, D), lambda i: (i, 0)), pl.BlockSpec((D, tm), lambda i: (0, i))],
        out_shape=[jax.ShapeDtypeStruct((T, D), F32), jax.ShapeDtypeStruct((D, T), BF16)],
        compiler_params=_cparams(("parallel",)),
        name="outproj",
    )(x2d, ym, ya, wo, g2)


def _peer_keys_kernel(keys_ref, wqt_ref, out_ref):
    out_ref[...] = jnp.dot(keys_ref[...], wqt_ref[...], preferred_element_type=F32,
                           precision=HI).astype(out_ref.dtype)


def _peer_keys(keys, wqt):
    _, H, K, C = keys.shape
    D = wqt.shape[-1]
    return pl.pallas_call(
        _peer_keys_kernel,
        grid=(2, H),
        in_specs=[pl.BlockSpec((None, None, K, C), lambda a, h: (a, h, 0, 0)),
                  pl.BlockSpec((None, None, C, D), lambda a, h: (a, h, 0, 0))],
        out_specs=pl.BlockSpec((None, None, K, D), lambda a, h: (a, h, 0, 0)),
        out_shape=jax.ShapeDtypeStruct((2, H, K, D), BF16),
        compiler_params=_cparams(("parallel", "parallel")),
        name="peer_keys",
    )(keys, wqt)


def _oddeven_merge_sort_pairs(n):
    pairs = []

    def merge(lo, hi, r):
        step = r * 2
        if step < hi - lo:
            merge(lo, hi, step)
            merge(lo + r, hi, step)
            for i in range(lo + r, hi - r, step):
                pairs.append((i, i + r))
        else:
            pairs.append((lo, lo + r))

    def sort(lo, hi):
        if hi - lo >= 1:
            mid = lo + (hi - lo) // 2
            sort(lo, mid)
            sort(mid + 1, hi)
            merge(lo, hi, 1)

    sort(0, n - 1)
    return pairs


_SORT16 = _oddeven_merge_sort_pairs(PEER_TOPK)


def _sort_desc(w):
    w = list(w)
    for a, b in _SORT16:
        hi, lo = jnp.maximum(w[a], w[b]), jnp.minimum(w[a], w[b])
        w[a], w[b] = hi, lo
    return w


def _merge_top(a, b):
    n = len(a)
    w = [jnp.maximum(a[i], b[n - 1 - i]) for i in range(n)]
    half = n // 2
    while half >= 1:
        for start in range(0, n, 2 * half):
            for i in range(start, start + half):
                hi, lo = jnp.maximum(w[i], w[i + half]), jnp.minimum(w[i], w[i + half])
                w[i], w[i + half] = hi, lo
        half //= 2
    return w


def _peer_topk_kernel(wk_ref, h2t_ref, ub_ref, nb_ref, vb_ref, rk_ref, sc_ref):
    H, K = PEER_HEADS, N_KEYS
    tb = h2t_ref.shape[1]
    h2t = h2t_ref[...]
    for a in range(2):
        for h in range(H):
            sc_ref[a, h] = _dot(wk_ref[a, h], h2t)

    sub = lax.broadcasted_iota(jnp.int32, (SUBLANES, LANES), 0)
    zeros = jnp.zeros((SUBLANES, LANES), F32)
    for lt in range(tb // LANES):
        ls = slice(lt * LANES, (lt + 1) * LANES)
        packed = []
        for a in range(2):
            acc = [None] * PEER_TOPK
            for h in range(H):
                w = [sc_ref[a, h, v * SUBLANES:(v + 1) * SUBLANES, ls] for v in range(K // SUBLANES)]
                w = _sort_desc(w)
                for shift in (4, 2, 1):
                    w = _merge_top(w, [pltpu.roll(x, shift, 0) for x in w])
                for i in range(PEER_TOPK):
                    acc[i] = w[i] if h == 0 else jnp.where(sub == h, w[i], acc[i])
            packed.append(acc)
        v1, v2 = packed
        rows = [[v1[a] + v2[b] for b in range(PEER_TOPK // (a + 1))] for a in range(PEER_TOPK)]
        neg = jnp.full((SUBLANES, LANES), -jnp.inf, F32)
        l0 = rows[0]
        l1 = _sort_desc(rows[1] + rows[2] + rows[4])
        l2 = _sort_desc(rows[3] + rows[5] + rows[6] + rows[7] + [rows[a][0] for a in range(8, 14)])
        l3 = [rows[14][0], rows[15][0]]
        l3 = [jnp.maximum(l3[0], l3[1]), jnp.minimum(l3[0], l3[1])] + [neg] * (PEER_TOPK - 2)
        vc = _merge_top(_merge_top(l0, l1), _merge_top(l2, l3))
        top, tau = vc[0], vc[PEER_TOPK - 1]
        zsum = zeros
        for i in range(PEER_TOPK):
            zsum = zsum + jnp.exp(vc[i] - top)
        inv_z = 1.0 / zsum
        n_sel = []
        for a in range(PEER_TOPK):
            cnt = zeros
            for cand in rows[a]:
                cnt = cnt + jnp.where(cand >= tau, 1.0, 0.0)
            n_sel.append(cnt)
        m1, m2 = v1[0], v2[0]
        for h in range(H):
            bc = lambda x: jnp.broadcast_to(x[h:h + 1, :], (SUBLANES, LANES))
            w1 = [bc(x) for x in v1]
            w2 = [bc(x) for x in v2]
            na = [bc(x) for x in n_sel]
            m1h, m2h, izh = bc(m1), bc(m2), bc(inv_z)
            ub, nb, vb, rk = [], [], [], []
            for v in range(K // SUBLANES):
                s1 = sc_ref[0, h, v * SUBLANES:(v + 1) * SUBLANES, ls]
                s2 = sc_ref[1, h, v * SUBLANES:(v + 1) * SUBLANES, ls]
                nbv, rkv = zeros, zeros
                for a in range(PEER_TOPK):
                    nbv = jnp.where(s1 == w1[a], na[a], nbv)
                    rkv = jnp.where(w2[a] > s2, float(a + 1), rkv)
                ub.append(jnp.exp(s1 - m1h) * izh * 0.5)
                vb.append(jnp.exp(s2 - m2h))
                nb.append(nbv)
                rk.append(rkv)
            ub_ref[h, :, ls] = jnp.concatenate(ub, axis=0)
            nb_ref[h, :, ls] = jnp.concatenate(nb, axis=0)
            vb_ref[h, :, ls] = jnp.concatenate(vb, axis=0).astype(BF16)
            rk_ref[h, :, ls] = jnp.concatenate(rk, axis=0).astype(BF16)


def _peer_topk(wk, h2t, tb):
    D, T = h2t.shape
    H, K = PEER_HEADS, N_KEYS
    spec = pl.BlockSpec((H, K, tb), lambda i: (0, 0, i))
    return pl.pallas_call(
        _peer_topk_kernel,
        grid=(T // tb,),
        in_specs=[pl.BlockSpec(wk.shape, lambda i: (0, 0, 0, 0)), pl.BlockSpec((D, tb), lambda i: (0, i))],
        out_specs=[spec] * 4,
        out_shape=[jax.ShapeDtypeStruct((H, K, T), F32)] * 2 + [jax.ShapeDtypeStruct((H, K, T), BF16)] * 2,
        scratch_shapes=[pltpu.VMEM((2, H, K, tb), F32)],
        compiler_params=_cparams(("parallel",)),
        name="peer_topk",
    )(wk, h2t)


def _peer_main_kernel(h2t_ref, ue_ref, vet_ref, ub_ref, nb_ref, vb_ref, rk_ref, x2_ref, out_ref,
                      acc_ref, a_ref):
    c = pl.program_id(1)
    n_i = ue_ref.shape[0] // N_KEYS

    @pl.when(c == 0)
    def _():
        acc_ref[...] = jnp.zeros_like(acc_ref)

    h2t = h2t_ref[...]
    for ii in range(n_i):
        rs = slice(ii * N_KEYS, (ii + 1) * N_KEYS)
        pre = _dot(ue_ref[rs, :], h2t)
        act = (pre * (lax.erf(pre * (2.0 ** -0.5)) + 1.0)).astype(BF16)
        gate = jnp.zeros(pre.shape, BF16)
        for h in range(PEER_HEADS):
            row = lambda ref: jnp.broadcast_to(ref[h, ii:ii + 1, :], pre.shape).astype(BF16)
            gate = gate + jnp.where(rk_ref[h] < row(nb_ref), row(ub_ref) * vb_ref[h], 0)
        a_ref[rs, :] = act * gate
    acc_ref[...] += _dot(vet_ref[...], a_ref[...])

    @pl.when(c == pl.num_programs(1) - 1)
    def _():
        out_ref[...] = x2_ref[...] + jnp.transpose(acc_ref[...])


def _peer_main(h2t, ue, vet, ub, nb, vb, rk, x2, tb, ec):
    D, T = h2t.shape
    E = ue.shape[0]
    H, K = PEER_HEADS, N_KEYS
    row = pl.BlockSpec((H, ec // K, tb), lambda t, c: (0, c, t))
    col = pl.BlockSpec((H, K, tb), lambda t, c: (0, 0, t))
    return pl.pallas_call(
        _peer_main_kernel,
        grid=(T // tb, E // ec),
        in_specs=[pl.BlockSpec((D, tb), lambda t, c: (0, t)),
                  pl.BlockSpec((ec, D), lambda t, c: (c, 0)),
                  pl.BlockSpec((D, ec), lambda t, c: (0, c)),
                  row, row, col, col,
                  pl.BlockSpec((tb, D), lambda t, c: (t, 0))],
        out_specs=pl.BlockSpec((tb, D), lambda t, c: (t, 0)),
        out_shape=jax.ShapeDtypeStruct((T, D), F32),
        scratch_shapes=[pltpu.VMEM((D, tb), F32), pltpu.VMEM((ec, tb), BF16)],
        compiler_params=_cparams(("parallel", "arbitrary")),
        name="peer_main",
    )(h2t, ue, vet, ub, nb, vb, rk, x2)


def _block_rows(T, want):
    tm = min(T, want)
    assert T % tm == 0
    return tm


PROJ_ROWS = 512
TOPK_TOKENS = 256
PEER_TOKENS = 512
PEER_EXPERTS = 2048


def _layer(x2d, B, S, norm1_g, w_in, conv_w, conv_b, wq_m, wk_m, ig_b, fg_b, mh_norm_g, skip_m,
           qn_g, kn_g, rel_bias, w_out, norm2_g, w_query, sub_keys1, sub_keys2, expert_u, expert_v):
    T, D = x2d.shape
    o_vm, o_z, o_i, o_f, o_q = M_WIDTH, 2 * M_WIDTH, 3 * M_WIDTH, 3 * M_WIDTH + M_HEADS, 3 * M_WIDTH + 2 * M_HEADS
    wm = w_in[:, :o_i].astype(BF16)
    wa = w_in[:, o_q:].astype(BF16)
    wgt = jnp.transpose(w_in[:, o_i:o_q]).astype(BF16)
    gb = jnp.concatenate([ig_b, fg_b])[:, None]
    seg = np.arange(A_WIDTH) // A_HEAD_DIM
    bd = jnp.asarray(seg[:, None] == seg[None, :], BF16)
    qkg = jnp.stack([qn_g.reshape(-1), kn_g.reshape(-1)])
    tm = _block_rows(T, PROJ_ROWS)
    uvz, qn, kn, va, grow = _inproj(x2d, norm1_g[None, :], wm, wa, wgt, gb, bd, qkg, tm)

    ym = _mlstm(uvz, grow, conv_w, conv_b[None, :], wq_m.astype(BF16), wk_m.astype(BF16),
                mh_norm_g.reshape(1, -1), skip_m.reshape(1, -1), B, S)

    ya = _dilattn(qn, kn, va, _attn_bias(rel_bias), B, S)

    x2, h2t = _outproj(x2d, ym, ya, w_out.astype(BF16), norm2_g[None, :], tm)

    keys = jnp.stack([sub_keys1, sub_keys2])
    wqt = jnp.transpose(w_query.reshape(D, PEER_HEADS, 2, PEER_QDIM // 2), (2, 1, 3, 0))
    wk = _peer_keys(keys, wqt)
    ub, nb, vb, rk = _peer_topk(wk, h2t, _block_rows(T, TOPK_TOKENS))
    ue = expert_u.astype(BF16)
    vet = jnp.transpose(expert_v).astype(BF16)
    return _peer_main(h2t, ue, vet, ub, nb, vb, rk, x2, _block_rows(T, PEER_TOKENS), PEER_EXPERTS)


def kernel(x, norm1_g, w_in, conv_w, conv_b, wq_m, wk_m, ig_b, fg_b, mh_norm_g, skip_m, qn_g, kn_g,
           rel_bias, w_out, norm2_g, w_query, sub_keys1, sub_keys2, expert_u, expert_v):
    B, S, D = x.shape
    assert S % DIL_PATTERNS[-1][0] == 0 and S % CHUNK == 0
    x2d = x.reshape(B * S, D)
    for l in range(norm1_g.shape[0]):
        x2d = _layer(x2d, B, S, norm1_g[l], w_in[l], conv_w[l], conv_b[l], wq_m[l], wk_m[l], ig_b[l],
                     fg_b[l], mh_norm_g[l], skip_m[l], qn_g[l], kn_g[l], rel_bias, w_out[l], norm2_g[l],
                     w_query[l], sub_keys1[l], sub_keys2[l], expert_u[l], expert_v[l])
    return x2d.reshape(B, S, D)
```

```python
import functools
import math

import numpy as np
import jax
import jax.numpy as jnp
from jax import lax
from jax.experimental import pallas as pl
from jax.experimental.pallas import tpu as pltpu

EPS = 1e-6
M_HEADS = 4
M_HEAD_DIM = 128
M_WIDTH = M_HEADS * M_HEAD_DIM
CONV_W = 4
CHUNK = 128
A_HEADS = 8
A_HEAD_DIM = 64
A_WIDTH = A_HEADS * A_HEAD_DIM
DIL_PATTERNS = ((128, 1), (512, 4), (2048, 16))
N_BUCKETS = 32
MAX_DISTANCE = 2048
PEER_HEADS = 8
N_KEYS = 128
PEER_QDIM = 256
PEER_TOPK = 16

LANES = 128
SUBLANES = 8
VMEM_LIMIT = 56 * 1024 * 1024

BF16 = jnp.bfloat16
F32 = jnp.float32
HI = lax.Precision.HIGHEST


def _cparams(sem):
    return pltpu.CompilerParams(dimension_semantics=sem, vmem_limit_bytes=VMEM_LIMIT)


def _dot(a, b):
    return jnp.dot(a, b, preferred_element_type=F32)


def _dot_nt(a, b):
    return lax.dot_general(a, b, (((1,), (1,)), ((), ())), preferred_element_type=F32)


def _dot_tn(a, b):
    return lax.dot_general(a, b, (((0,), (0,)), ((), ())), preferred_element_type=F32)


def _inproj_kernel(x_ref, g1_ref, wm_ref, wa_ref, wgt_ref, gb_ref, bd_ref, qkg_ref,
                   uvz_ref, q_ref, k_ref, v_ref, grow_ref):
    x = x_ref[...]
    h = x * lax.rsqrt(jnp.mean(x * x, axis=-1, keepdims=True) + EPS) * g1_ref[...]
    hb = h.astype(BF16)
    uvz_ref[...] = _dot(hb, wm_ref[...])
    qkv = _dot(hb, wa_ref[...])
    bd = bd_ref[...]

    def head_norm(t, g):
        ms = _dot((t * t).astype(BF16), bd) * (1.0 / A_HEAD_DIM)
        return t * lax.rsqrt(ms + EPS) * g

    q_ref[...] = head_norm(qkv[:, :A_WIDTH], qkg_ref[0:1, :])
    k_ref[...] = head_norm(qkv[:, A_WIDTH:2 * A_WIDTH], qkg_ref[1:2, :])
    v_ref[...] = qkv[:, 2 * A_WIDTH:]
    gr = _dot_nt(wgt_ref[...], hb) + gb_ref[...]
    lf = jnp.minimum(gr, 0.0) - jnp.log1p(jnp.exp(-jnp.abs(gr)))
    row = lax.broadcasted_iota(jnp.int32, gr.shape, 0)
    grow_ref[...] = jnp.where(row < M_HEADS, gr, lf)


def _inproj(x2d, g1, wm, wa, wgt, gb, bd, qkg, tm):
    T, D = x2d.shape
    full = lambda a: pl.BlockSpec(a.shape, lambda i: (0,) * a.ndim)
    return pl.pallas_call(
        _inproj_kernel,
        grid=(T // tm,),
        in_specs=[pl.BlockSpec((tm, D), lambda i: (i, 0)), full(g1), full(wm), full(wa), full(wgt),
                  full(gb), full(bd), full(qkg)],
        out_specs=[pl.BlockSpec((tm, 3 * M_WIDTH), lambda i: (i, 0)),
                   pl.BlockSpec((tm, A_WIDTH), lambda i: (i, 0)),
                   pl.BlockSpec((tm, A_WIDTH), lambda i: (i, 0)),
                   pl.BlockSpec((tm, A_WIDTH), lambda i: (i, 0)),
                   pl.BlockSpec((2 * M_HEADS, tm), lambda i: (0, i))],
        out_shape=[jax.ShapeDtypeStruct((T, 3 * M_WIDTH), F32),
                   jax.ShapeDtypeStruct((T, A_WIDTH), F32),
                   jax.ShapeDtypeStruct((T, A_WIDTH), F32),
                   jax.ShapeDtypeStruct((T, A_WIDTH), F32),
                   jax.ShapeDtypeStruct((2 * M_HEADS, T), F32)],
        compiler_params=_cparams(("parallel",)),
        name="inproj",
    )(x2d, g1, wm, wa, wgt, gb, bd, qkg)


def _mlstm_kernel(*refs):
    nb = (len(refs) - 8) // (2 + 2 * M_HEADS)
    uvz_all, grow_refs = refs[0], refs[1:1 + nb]
    cw_ref, cb_ref, wq_ref, wk_ref, ng_ref, sk_ref, ym_all = refs[1 + nb:8 + nb]
    ubuf_refs = refs[8 + nb:8 + 2 * nb]
    cst_refs = refs[8 + 2 * nb:8 + 2 * nb + nb * M_HEADS]
    mst_refs = refs[8 + 2 * nb + nb * M_HEADS:]
    L = CHUNK

    @pl.when(pl.program_id(0) == 0)
    def _():
        for r in ubuf_refs:
            r[0:SUBLANES, :] = jnp.zeros((SUBLANES, M_WIDTH), F32)
        for r in cst_refs + mst_refs:
            r[...] = jnp.zeros_like(r)

    rr = lax.broadcasted_iota(jnp.int32, (L, L), 0)
    cc = lax.broadcasted_iota(jnp.int32, (L, L), 1)
    causal = cc <= rr
    lane8 = lax.broadcasted_iota(jnp.int32, (2 * M_HEADS, L), 1)
    seqs = range(nb)
    chains = [(b, h) for b in seqs for h in range(M_HEADS)]
    cols = lambda h, off=0: slice(off + h * M_HEAD_DIM, off + (h + 1) * M_HEAD_DIM)

    cact, gr, b_rows = {}, {}, {}
    for b in seqs:
        uvz_ref, ubuf_ref = uvz_all.at[b], ubuf_refs[b]
        u = uvz_ref[:, 0:M_WIDTH]
        ubuf_ref[SUBLANES:SUBLANES + L, :] = u
        conv = cb_ref[...] + jnp.zeros((L, M_WIDTH), F32)
        for w in range(CONV_W):
            conv = conv + ubuf_ref[pl.ds(SUBLANES - (CONV_W - 1) + w, L), :] * cw_ref[w:w + 1, :]
        ubuf_ref[0:SUBLANES, :] = u[L - SUBLANES:, :]
        cact[b] = conv * jax.nn.sigmoid(conv)
        gr[b] = grow_refs[b][...]
    ch, qb, kb, kf, vaug = {}, {}, {}, {}, {}
    for c in chains:
        b, h = c
        ch[c] = cact[b][:, cols(h)]
        chb = ch[c].astype(BF16)
        qb[c] = _dot(chb, wq_ref[h]).astype(BF16)
        kf[c] = _dot(chb, wk_ref[h]) * (M_HEAD_DIM ** -0.5)
        kb[c] = kf[c].astype(BF16)
        v = uvz_all[b, :, cols(h, M_WIDTH)]
        vaug[c] = jnp.concatenate([v, jnp.ones((L, M_HEAD_DIM), F32)], axis=1).astype(BF16)

    for b in seqs:
        b_rows[b] = gr[b]
    shift = 1
    while shift < L:
        for b in seqs:
            b_rows[b] = b_rows[b] + jnp.where(lane8 >= shift, pltpu.roll(b_rows[b], shift, 1), 0.0)
        shift *= 2
    b_up = {b: pltpu.roll(b_rows[b], M_HEADS, 0) for b in seqs}
    rmax = {b: gr[b] - b_up[b] for b in seqs}
    shift = 1
    while shift < L:
        for b in seqs:
            rmax[b] = jnp.maximum(rmax[b], jnp.where(lane8 >= shift, pltpu.roll(rmax[b], shift, 1), -jnp.inf))
        shift *= 2
    pieces = {}
    for b in seqs:
        xs = jnp.concatenate([b_rows[b], gr[b], rmax[b] + b_up[b]], axis=0)
        hi = xs.astype(BF16)
        r1 = xs - hi.astype(F32)
        mid = r1.astype(BF16)
        pieces[b] = (hi, mid, (r1 - mid.astype(F32)).astype(BF16))
    sel_r = lax.broadcasted_iota(jnp.int32, (3 * 2 * M_HEADS, 3 * LANES), 0)
    sel_c = lax.broadcasted_iota(jnp.int32, (3 * 2 * M_HEADS, 3 * LANES), 1)

    b_c, li_c, b_last, m_prev, caug, m_t, d_mat, inter_w = {}, {}, {}, {}, {}, {}, {}, {}
    for c in chains:
        b, h = c
        want = jnp.where(sel_c < LANES, M_HEADS + h, jnp.where(sel_c < 2 * LANES, 2 * M_HEADS + h, 4 * M_HEADS + h))
        sel = (sel_r == want).astype(BF16)
        rep = _dot_tn(pieces[b][0], sel) + _dot_tn(pieces[b][1], sel) + _dot_tn(pieces[b][2], sel)
        b_c[c], li_c[c], row_max = rep[:, :LANES], rep[:, LANES:2 * LANES], rep[:, 2 * LANES:]
        b_r = b_rows[b][M_HEADS + h:M_HEADS + h + 1, :]
        li_r = gr[b][h:h + 1, :]
        b_last[c] = b_c[c][L - 1:L, :]
        m_prev[c] = mst_refs[b * M_HEADS + h][0:1, :]
        caug[c] = cst_refs[b * M_HEADS + h][...]
        log_d = b_c[c] - b_r + li_r
        inter = b_c[c] + m_prev[c]
        m_t[c] = jnp.maximum(inter, row_max)
        d_mat[c] = jnp.where(causal, jnp.exp(log_d - m_t[c]), 0.0)
        inter_w[c] = jnp.exp(inter - m_t[c])

    twice = lambda x: jnp.concatenate([x, x], axis=1)
    s = {c: _dot_nt(qb[c], kb[c]) * d_mat[c] for c in chains}
    qc = {c: _dot(qb[c], caug[c].astype(BF16)) for c in chains}
    nd = {c: _dot(s[c].astype(BF16), vaug[c]) + twice(inter_w[c]) * qc[c] for c in chains}

    for c in chains:
        b, h = c
        g_c = b_last[c] - b_c[c] + li_c[c]
        m_loc = jnp.max(g_c, axis=0, keepdims=True)
        kw = (jnp.exp(g_c - m_loc) * kf[c]).astype(BF16)
        c_new = _dot_tn(kw, vaug[c])
        m_new = jnp.maximum(b_last[c] + m_prev[c], m_loc)
        a = jnp.exp(b_last[c] + m_prev[c] - m_new)
        bb = jnp.exp(m_loc - m_new)
        cst_refs[b * M_HEADS + h][...] = twice(a) * caug[c] + twice(bb) * c_new
        mst_refs[b * M_HEADS + h][...] = jnp.broadcast_to(m_new, (SUBLANES, LANES))

    mean_mat = jnp.full((M_HEAD_DIM, M_HEAD_DIM), 1.0 / M_HEAD_DIM, BF16)
    for c in chains:
        b, h = c
        num, den = nd[c][:, :M_HEAD_DIM], nd[c][:, M_HEAD_DIM:]
        hval = num / jnp.maximum(jnp.abs(den), jnp.exp(-m_t[c]))
        hn = hval * lax.rsqrt(_dot((hval * hval).astype(BF16), mean_mat) + EPS) * ng_ref[:, cols(h)]
        hn = hn + sk_ref[:, cols(h)] * ch[c]
        z = uvz_all[b, :, cols(h, 2 * M_WIDTH)]
        ym_all[b, :, cols(h)] = jax.nn.sigmoid(z) * hn


def _mlstm(uvz, grow, cw, cb, wq, wk, ng, sk, B, S):
    nc = S // CHUNK
    full = lambda a: pl.BlockSpec(a.shape, lambda c: (0,) * a.ndim)
    gate_specs = [pl.BlockSpec((2 * M_HEADS, CHUNK), lambda c, b=b: (0, b * nc + c)) for b in range(B)]
    ym = pl.pallas_call(
        _mlstm_kernel,
        grid=(nc,),
        in_specs=[pl.BlockSpec((B, CHUNK, 3 * M_WIDTH), lambda c: (0, c, 0))] + gate_specs
                 + [full(cw), full(cb), full(wq), full(wk), full(ng), full(sk)],
        out_specs=pl.BlockSpec((B, CHUNK, M_WIDTH), lambda c: (0, c, 0)),
        out_shape=jax.ShapeDtypeStruct((B, S, M_WIDTH), F32),
        scratch_shapes=([pltpu.VMEM((SUBLANES + CHUNK, M_WIDTH), F32)] * B
                        + [pltpu.VMEM((M_HEAD_DIM, 2 * M_HEAD_DIM), F32)] * (B * M_HEADS)
                        + [pltpu.VMEM((SUBLANES, LANES), F32)] * (B * M_HEADS)),
        compiler_params=_cparams(("arbitrary",)),
        name="mlstm",
    )(uvz.reshape(B, S, 3 * M_WIDTH), *([grow] * B), cw, cb, wq, wk, ng, sk)
    return ym.reshape(B * S, M_WIDTH)


ATT_BLK = DIL_PATTERNS[0][0] // DIL_PATTERNS[0][1]
ATT_SPAN = DIL_PATTERNS[-1][0]


ATT_R = DIL_PATTERNS[1][1]
assert [d for _, d in DIL_PATTERNS] == [1, ATT_R, ATT_R * ATT_R]
ATT_UNROLL = 16


def _dilattn_kernel(q_ref, kp_ref, kc_ref, vp_ref, vc_ref, bias_ref, o_ref,
                    q4_ref, k4_ref, v4_ref, k1_ref, v1_ref, a1_ref, a4_ref):
    blk, span, R = ATT_BLK, ATT_SPAN, ATT_R
    sub = span // R
    n = pl.program_id(1)
    scale = A_HEAD_DIM ** -0.5
    for r in range(R):
        res = pl.ds(r, sub, stride=R)
        q4_ref[r] = q_ref[res, :] * scale
        k4_ref[r, 0:sub] = kp_ref[res, :]
        k4_ref[r, sub:] = kc_ref[res, :]
        v4_ref[r, 0:sub] = vp_ref[res, :]
        v4_ref[r, sub:] = vc_ref[res, :]
    k1_ref[0:blk] = kp_ref[span - blk:, :]
    k1_ref[blk:] = kc_ref[...]
    v1_ref[0:blk] = vp_ref[span - blk:, :]
    v1_ref[blk:] = vc_ref[...]
    first_q =lax.broadcasted_iota(jnp.int32, (blk, LANES), 1) < A_HEAD_DIM

    def attend(q, kk, vv, bias_sel):
        kb = kk.astype(BF16)
        vb = jnp.concatenate([vv, jnp.ones(vv.shape, F32)], axis=1).astype(BF16)
        ms, ls, os_ = [], [], []
        for s_ in range(2):
            qh = jnp.where(first_q, q, 0.0) if s_ == 0 else jnp.where(first_q, 0.0, q)
            s = _dot_nt(qh.astype(BF16), kb) + bias_sel(s_)
            m = jnp.broadcast_to(jnp.max(s, axis=1, keepdims=True), (blk, LANES))
            e = jnp.exp(s - jnp.concatenate([m, m], axis=1))
            ol = _dot(e.astype(BF16), vb)
            ms.append(m)
            os_.append(ol[:, :LANES])
            ls.append(ol[:, LANES:])
        return (jnp.where(first_q, ms[0], ms[1]), jnp.where(first_q, ls[0], ls[1]),
                jnp.where(first_q, os_[0], os_[1]))

    def unit_d1(m, carry):
        base = pl.multiple_of(m * blk, blk)
        first = jnp.logical_and(n == 0, m == 0).astype(jnp.int32)
        m_c, l_c, o_c = attend(q_ref[pl.ds(base, blk), :] * scale, k1_ref[pl.ds(base, 2 * blk), :],
                               v1_ref[pl.ds(base, 2 * blk), :], lambda s_: bias_ref[0, first, s_])
        a1_ref[0, pl.ds(base, blk), :] = m_c
        a1_ref[1, pl.ds(base, blk), :] = l_c
        a1_ref[2, pl.ds(base, blk), :] = o_c
        return carry

    def unit_d4(u, carry):
        m4, r = u >> (R.bit_length() - 1), u & (R - 1)
        off = pl.multiple_of(m4 * blk, blk)
        first = jnp.logical_and(n == 0, m4 == 0).astype(jnp.int32)
        keys = pl.ds(sub - blk + off, 2 * blk)
        m_c, l_c, o_c = attend(q4_ref[r, pl.ds(off, blk), :], k4_ref[r, keys, :], v4_ref[r, keys, :],
                               lambda s_: bias_ref[1, first, s_])
        a4_ref[0, r, pl.ds(off, blk), :] = m_c
        a4_ref[1, r, pl.ds(off, blk), :] = l_c
        a4_ref[2, r, pl.ds(off, blk), :] = o_c
        return carry

    def unit_d16(u, carry):
        r, rp = u & (R - 1), u >> (R.bit_length() - 1)
        first = (n == 0).astype(jnp.int32)
        own, cur = pl.ds(rp, blk, stride=R), pl.ds(sub + rp, blk, stride=R)
        kk = jnp.concatenate([k4_ref[r, own, :], k4_ref[r, cur, :]], axis=0)
        vv = jnp.concatenate([v4_ref[r, own, :], v4_ref[r, cur, :]], axis=0)
        m_c, l_c, o_c = attend(q4_ref[r, own, :], kk, vv, lambda s_: bias_ref[2, first, s_])
        m_o = a4_ref[0, r, own, :]
        m_n = jnp.maximum(m_o, m_c)
        a_o, a_c = jnp.exp(m_o - m_n), jnp.exp(m_c - m_n)
        a4_ref[0, r, own, :] = m_n
        a4_ref[1, r, own, :] = a_o * a4_ref[1, r, own, :] + a_c * l_c
        a4_ref[2, r, own, :] = a_o * a4_ref[2, r, own, :] + a_c * o_c
        return carry

    units = span // blk
    lax.fori_loop(0, units, unit_d1, 0, unroll=ATT_UNROLL)
    lax.fori_loop(0, units, unit_d4, 0, unroll=ATT_UNROLL)
    lax.fori_loop(0, units, unit_d16, 0, unroll=ATT_UNROLL)

    for r in range(R):
        res = pl.ds(r, sub, stride=R)
        m1, m4 = a1_ref[0, res, :], a4_ref[0, r]
        m_n = jnp.maximum(m1, m4)
        e1, e4 = jnp.exp(m1 - m_n), jnp.exp(m4 - m_n)
        o_ref[res, :] = ((e1 * a1_ref[2, res, :] + e4 * a4_ref[2, r])
                         / (e1 * a1_ref[1, res, :] + e4 * a4_ref[1, r]))


def _dilattn(q, k, v, bias, B, S):
    span, blk, R = ATT_SPAN, ATT_BLK, ATT_R
    ns = S // span
    T = B * S
    cur = pl.BlockSpec((span, LANES), lambda b, n, hp: (b * ns + n, hp))
    prev = pl.BlockSpec((span, LANES), lambda b, n, hp: (b * ns + jnp.maximum(n - 1, 0), hp))
    return pl.pallas_call(
        _dilattn_kernel,
        grid=(B, ns, A_HEADS // 2),
        in_specs=[cur, prev, cur, prev, cur,
                  pl.BlockSpec((len(DIL_PATTERNS), 2, 2, blk, 2 * blk), lambda b, n, hp: (0, 0, hp, 0, 0))],
        out_specs=cur,
        out_shape=jax.ShapeDtypeStruct((T, A_WIDTH), F32),
        scratch_shapes=[pltpu.VMEM((R, span // R, LANES), F32),
                        pltpu.VMEM((R, 2 * span // R, LANES), F32),
                        pltpu.VMEM((R, 2 * span // R, LANES), F32),
                        pltpu.VMEM((blk + span, LANES), F32),
                        pltpu.VMEM((blk + span, LANES), F32),
                        pltpu.VMEM((3, span, LANES), F32),
                        pltpu.VMEM((3, R, span // R, LANES), F32)],
        compiler_params=_cparams(("parallel", "parallel", "parallel")),
        name="dilattn",
    )(q, k, k, v, v, bias)


def _attn_bias_kernel(bkt_ref, rb_ref, out_ref):
    h = pl.program_id(1)
    bkt = bkt_ref[...]
    acc = jnp.full(bkt.shape, -jnp.inf, F32)
    for kk in range(N_BUCKETS):
        acc = jnp.where(bkt == kk, rb_ref[kk, h], acc)
    col = lax.broadcasted_iota(jnp.int32, bkt.shape, 1)
    out_ref[0] = acc
    out_ref[1] = jnp.where(col < ATT_BLK, -jnp.inf, acc)


def _attn_bias(rel_bias):
    blk = ATT_BLK
    i = jnp.arange(blk)[:, None]
    j = jnp.arange(2 * blk)[None, :]
    steps = i + blk - j
    band = (steps >= 0) & (steps <= blk)
    bkt = jnp.stack([jnp.where(band, _t5_bucket(jnp.maximum(steps, 0) * d), -1) for _, d in DIL_PATTERNS])
    P = len(DIL_PATTERNS)
    return pl.pallas_call(
        _attn_bias_kernel,
        grid=(P, A_HEADS),
        in_specs=[pl.BlockSpec((None, blk, 2 * blk), lambda p, h: (p, 0, 0)),
                  pl.BlockSpec(memory_space=pltpu.SMEM)],
        out_specs=pl.BlockSpec((None, 2, None, blk, 2 * blk), lambda p, h: (p, 0, h, 0, 0)),
        out_shape=jax.ShapeDtypeStruct((P, 2, A_HEADS, blk, 2 * blk), F32),
        compiler_params=_cparams(("parallel", "parallel")),
        name="attn_bias",
    )(bkt.astype(jnp.int32), rel_bias)


def _t5_bucket(dist):
    max_exact = N_BUCKETS // 2
    nf = jnp.maximum(dist, 1).astype(F32)
    large = max_exact + (jnp.log(nf / max_exact) / np.log(MAX_DISTANCE / max_exact)
                         * (N_BUCKETS - max_exact)).astype(jnp.int32)
    large = jnp.minimum(large, N_BUCKETS - 1)
    return jnp.where(dist < max_exact, dist, large)


def _outproj_kernel(x_ref, ym_ref, ya_ref, wo_ref, g2_ref, x2_ref, h2t_ref):
    y = (_dot(ym_ref[...].astype(BF16), wo_ref[0:M_WIDTH, :])
         + _dot(ya_ref[...].astype(BF16), wo_ref[M_WIDTH:, :]))
    x2 = x_ref[...] + y
    x2_ref[...] = x2
    h2 = x2 * lax.rsqrt(jnp.mean(x2 * x2, axis=-1, keepdims=True) + EPS) * g2_ref[...]
    h2t_ref[...] = jnp.transpose(h2).astype(BF16)


def _outproj(x2d, ym, ya, wo, g2, tm):
    T, D = x2d.shape
    half = pl.BlockSpec((tm, M_WIDTH), lambda i: (i, 0))
    full = lambda a: pl.BlockSpec(a.shape, lambda i: (0,) * a.ndim)
    return pl.pallas_call(
        _outproj_kernel,
        grid=(T // tm,),
        in_specs=[pl.BlockSpec((tm, D), lambda i: (i, 0)), half, half, full(wo), full(g2)],
        out_specs=[pl.BlockSpec((tm, D), lambda i: (i, 0)), pl.BlockSpec((D, tm), lambda i: (0, i))],
        out_shape=[jax.ShapeDtypeStruct((T, D), F32), jax.ShapeDtypeStruct((D, T), BF16)],
        compiler_params=_cparams(("parallel",)),
        name="outproj",
    )(x2d, ym, ya, wo, g2)


def _peer_keys_kernel(keys_ref, wqt_ref, out_ref):
    out_ref[...] = jnp.dot(keys_ref[...], wqt_ref[...], preferred_element_type=F32,
                           precision=HI).astype(out_ref.dtype)


def _peer_keys(keys, wqt):
    _, H, K, C = keys.shape
    D = wqt.shape[-1]
    return pl.pallas_call(
        _peer_keys_kernel,
        grid=(2, H),
        in_specs=[pl.BlockSpec((None, None, K, C), lambda a, h: (a, h, 0, 0)),
                  pl.BlockSpec((None, None, C, D), lambda a, h: (a, h, 0, 0))],
        out_specs=pl.BlockSpec((None, None, K, D), lambda a, h: (a, h, 0, 0)),
        out_shape=jax.ShapeDtypeStruct((2, H, K, D), BF16),
        compiler_params=_cparams(("parallel", "parallel")),
        name="peer_keys",
    )(keys, wqt)


def _oddeven_merge_sort_pairs(n):
    pairs = []

    def merge(lo, hi, r):
        step = r * 2
        if step < hi - lo:
            merge(lo, hi, step)
            merge(lo + r, hi, step)
            for i in range(lo + r, hi - r, step):
                pairs.append((i, i + r))
        else:
            pairs.append((lo, lo + r))

    def sort(lo, hi):
        if hi - lo >= 1:
            mid = lo + (hi - lo) // 2
            sort(lo, mid)
            sort(mid + 1, hi)
            merge(lo, hi, 1)

    sort(0, n - 1)
    return pairs


_SORT16 = _oddeven_merge_sort_pairs(PEER_TOPK)


def _sort_desc(w):
    w = list(w)
    for a, b in _SORT16:
        hi, lo = jnp.maximum(w[a], w[b]), jnp.minimum(w[a], w[b])
        w[a], w[b] = hi, lo
    return w


def _merge_top(a, b):
    n = len(a)
    w = [jnp.maximum(a[i], b[n - 1 - i]) for i in range(n)]
    half = n // 2
    while half >= 1:
        for start in range(0, n, 2 * half):
            for i in range(start, start + half):
                hi, lo = jnp.maximum(w[i], w[i + half]), jnp.minimum(w[i], w[i + half])
                w[i], w[i + half] = hi, lo
        half //= 2
    return w


def _peer_topk_kernel(wk_ref, h2t_ref, ub_ref, nb_ref, vb_ref, rk_ref, sc_ref):
    H, K = PEER_HEADS, N_KEYS
    tb = h2t_ref.shape[1]
    h2t = h2t_ref[...]
    for a in range(2):
        for h in range(H):
            sc_ref[a, h] = _dot(wk_ref[a, h], h2t)

    sub = lax.broadcasted_iota(jnp.int32, (SUBLANES, LANES), 0)
    zeros = jnp.zeros((SUBLANES, LANES), F32)
    for lt in range(tb // LANES):
        ls = slice(lt * LANES, (lt + 1) * LANES)
        packed = []
        for a in range(2):
            acc = [None] * PEER_TOPK
            for h in range(H):
                w = [sc_ref[a, h, v * SUBLANES:(v + 1) * SUBLANES, ls] for v in range(K // SUBLANES)]
                w = _sort_desc(w)
                for shift in (4, 2, 1):
                    w = _merge_top(w, [pltpu.roll(x, shift, 0) for x in w])
                for i in range(PEER_TOPK):
                    acc[i] = w[i] if h == 0 else jnp.where(sub == h, w[i], acc[i])
            packed.append(acc)
        v1, v2 = packed
        rows = [[v1[a] + v2[b] for b in range(PEER_TOPK // (a + 1))] for a in range(PEER_TOPK)]
        neg = jnp.full((SUBLANES, LANES), -jnp.inf, F32)
        l0 = rows[0]
        l1 = _sort_desc(rows[1] + rows[2] + rows[4])
        l2 = _sort_desc(rows[3] + rows[5] + rows[6] + rows[7] + [rows[a][0] for a in range(8, 14)])
        l3 = [rows[14][0], rows[15][0]]
        l3 = [jnp.maximum(l3[0], l3[1]), jnp.minimum(l3[0], l3[1])] + [neg] * (PEER_TOPK - 2)
        vc = _merge_top(_merge_top(l0, l1), _merge_top(l2, l3))
        top, tau = vc[0], vc[PEER_TOPK - 1]
        zsum = zeros
        for i in range(PEER_TOPK):
            zsum = zsum + jnp.exp(vc[i] - top)
        inv_z = 1.0 / zsum
        n_sel = []
        for a in range(PEER_TOPK):
            cnt = zeros
            for cand in rows[a]:
                cnt = cnt + jnp.where(cand >= tau, 1.0, 0.0)
            n_sel.append(cnt)
        m1, m2 = v1[0], v2[0]
        for h in range(H):
            bc = lambda x: jnp.broadcast_to(x[h:h + 1, :], (SUBLANES, LANES))
            w1 = [bc(x) for x in v1]
            w2 = [bc(x) for x in v2]
            na = [bc(x) for x in n_sel]
            m1h, m2h, izh = bc(m1), bc(m2), bc(inv_z)
            ub, nb, vb, rk = [], [], [], []
            for v in range(K // SUBLANES):
                s1 = sc_ref[0, h, v * SUBLANES:(v + 1) * SUBLANES, ls]
                s2 = sc_ref[1, h, v * SUBLANES:(v + 1) * SUBLANES, ls]
                nbv = zeros
                for a in range(PEER_TOPK):
                    nbv = jnp.where(s1 == w1[a], na[a], nbv)
                rkv, step, taken = zeros, PEER_TOPK // 2, []
                while step >= 1:
                    level = w2[step - 1:PEER_TOPK - 1:2 * step]
                    for bit in reversed(taken):
                        level = [jnp.where(bit, hi_, lo_) for lo_, hi_ in zip(level[0::2], level[1::2])]
                    above = level[0] > s2
                    taken.append(above)
                    rkv = rkv + jnp.where(above, float(step), 0.0)
                    step //= 2
                rkv = jnp.where(w2[PEER_TOPK - 1] > s2, float(PEER_TOPK), rkv)
                ub.append(jnp.exp(s1 - m1h) * izh * 0.5)
                vb.append(jnp.exp(s2 - m2h))
                nb.append(nbv)
                rk.append(rkv)
            ub_ref[h, :, ls] = jnp.concatenate(ub, axis=0)
            nb_ref[h, :, ls] = jnp.concatenate(nb, axis=0)
            vb_ref[h, :, ls] = jnp.concatenate(vb, axis=0).astype(BF16)
            rk_ref[h, :, ls] = jnp.concatenate(rk, axis=0).astype(BF16)


def _peer_topk(wk, h2t, tb):
    D, T = h2t.shape
    H, K = PEER_HEADS, N_KEYS
    spec = pl.BlockSpec((H, K, tb), lambda i: (0, 0, i))
    return pl.pallas_call(
        _peer_topk_kernel,
        grid=(T // tb,),
        in_specs=[pl.BlockSpec(wk.shape, lambda i: (0, 0, 0, 0)), pl.BlockSpec((D, tb), lambda i: (0, i))],
        out_specs=[spec] * 4,
        out_shape=[jax.ShapeDtypeStruct((H, K, T), F32)] * 2 + [jax.ShapeDtypeStruct((H, K, T), BF16)] * 2,
        scratch_shapes=[pltpu.VMEM((2, H, K, tb), F32)],
        compiler_params=_cparams(("parallel",)),
        name="peer_topk",
    )(wk, h2t)


def _peer_main_kernel(h2t_ref, ue_ref, vet_ref, ub_ref, nb_ref, vb_ref, rk_ref, x2_ref, out_ref,
                      acc_ref, a_ref):
    c = pl.program_id(1)
    n_i = ue_ref.shape[0] // N_KEYS

    @pl.when(c == 0)
    def _():
        acc_ref[...] = jnp.zeros_like(acc_ref)

    h2t = h2t_ref[...]
    for ii in range(n_i):
        rs = slice(ii * N_KEYS, (ii + 1) * N_KEYS)
        pre = _dot(ue_ref[rs, :], h2t)
        act = (pre * (lax.erf(pre * (2.0 ** -0.5)) + 1.0)).astype(BF16)
        gate = jnp.zeros(pre.shape, BF16)
        for h in range(PEER_HEADS):
            row = lambda ref: jnp.broadcast_to(ref[h, ii:ii + 1, :], pre.shape).astype(BF16)
            gate = gate + jnp.where(rk_ref[h] < row(nb_ref), row(ub_ref) * vb_ref[h], 0)
        a_ref[rs, :] = act * gate
    acc_ref[...] += _dot(vet_ref[...], a_ref[...])

    @pl.when(c == pl.num_programs(1) - 1)
    def _():
        out_ref[...] = x2_ref[...] + jnp.transpose(acc_ref[...])


def _peer_main(h2t, ue, vet, ub, nb, vb, rk, x2, tb, ec):
    D, T = h2t.shape
    E = ue.shape[0]
    H, K = PEER_HEADS, N_KEYS
    row = pl.BlockSpec((H, ec // K, tb), lambda t, c: (0, c, t))
    col = pl.BlockSpec((H, K, tb), lambda t, c: (0, 0, t))
    return pl.pallas_call(
        _peer_main_kernel,
        grid=(T // tb, E // ec),
        in_specs=[pl.BlockSpec((D, tb), lambda t, c: (0, t)),
                  pl.BlockSpec((ec, D), lambda t, c: (c, 0)),
                  pl.BlockSpec((D, ec), lambda t, c: (0, c)),
                  row, row, col, col,
                  pl.BlockSpec((tb, D), lambda t, c: (t, 0))],
        out_specs=pl.BlockSpec((tb, D), lambda t, c: (t, 0)),
        out_shape=jax.ShapeDtypeStruct((T, D), F32),
        scratch_shapes=[pltpu.VMEM((D, tb), F32), pltpu.VMEM((ec, tb), BF16)],
        compiler_params=_cparams(("parallel", "arbitrary")),
        name="peer_main",
    )(h2t, ue, vet, ub, nb, vb, rk, x2)


def _block_rows(T, want):
    tm = min(T, want)
    assert T % tm == 0
    return tm


PROJ_ROWS = 512
TOPK_TOKENS = 256
PEER_TOKENS = 512
PEER_EXPERTS = 2048


def _layer(x2d, B, S, norm1_g, w_in, conv_w, conv_b, wq_m, wk_m, ig_b, fg_b, mh_norm_g, skip_m,
           qn_g, kn_g, rel_bias, w_out, norm2_g, w_query, sub_keys1, sub_keys2, expert_u, expert_v):
    T, D = x2d.shape
    o_vm, o_z, o_i, o_f, o_q = M_WIDTH, 2 * M_WIDTH, 3 * M_WIDTH, 3 * M_WIDTH + M_HEADS, 3 * M_WIDTH + 2 * M_HEADS
    wm = w_in[:, :o_i].astype(BF16)
    wa = w_in[:, o_q:].astype(BF16)
    wgt = jnp.transpose(w_in[:, o_i:o_q]).astype(BF16)
    gb = jnp.concatenate([ig_b, fg_b])[:, None]
    seg = np.arange(A_WIDTH) // A_HEAD_DIM
    bd = jnp.asarray(seg[:, None] == seg[None, :], BF16)
    qkg = jnp.stack([qn_g.reshape(-1), kn_g.reshape(-1)])
    tm = _block_rows(T, PROJ_ROWS)
    uvz, qn, kn, va, grow = _inproj(x2d, norm1_g[None, :], wm, wa, wgt, gb, bd, qkg, tm)

    ym = _mlstm(uvz, grow, conv_w, conv_b[None, :], wq_m.astype(BF16), wk_m.astype(BF16),
                mh_norm_g.reshape(1, -1), skip_m.reshape(1, -1), B, S)

    ya = _dilattn(qn, kn, va, _attn_bias(rel_bias), B, S)

    x2, h2t = _outproj(x2d, ym, ya, w_out.astype(BF16), norm2_g[None, :], tm)

    keys = jnp.stack([sub_keys1, sub_keys2])
    wqt = jnp.transpose(w_query.reshape(D, PEER_HEADS, 2, PEER_QDIM // 2), (2, 1, 3, 0))
    wk = _peer_keys(keys, wqt)
    ub, nb, vb, rk = _peer_topk(wk, h2t, _block_rows(T, TOPK_TOKENS))
    ue = expert_u.astype(BF16)
    vet = jnp.transpose(expert_v).astype(BF16)
    return _peer_main(h2t, ue, vet, ub, nb, vb, rk, x2, _block_rows(T, PEER_TOKENS), PEER_EXPERTS)


def kernel(x, norm1_g, w_in, conv_w, conv_b, wq_m, wk_m, ig_b, fg_b, mh_norm_g, skip_m, qn_g, kn_g,
           rel_bias, w_out, norm2_g, w_query, sub_keys1, sub_keys2, expert_u, expert_v):
    B, S, D = x.shape
    assert S % DIL_PATTERNS[-1][0] == 0 and S % CHUNK == 0
    x2d = x.reshape(B * S, D)
    for l in range(norm1_g.shape[0]):
        x2d = _layer(x2d, B, S, norm1_g[l], w_in[l], conv_w[l], conv_b[l], wq_m[l], wk_m[l], ig_b[l],
                     fg_b[l], mh_norm_g[l], skip_m[l], qn_g[l], kn_g[l], rel_bias, w_out[l], norm2_g[l],
                     w_query[l], sub_keys1[l], sub_keys2[l], expert_u[l], expert_v[l])
    return x2d.reshape(B, S, D)
```

```python
import functools
import math

import numpy as np
import jax
import jax.numpy as jnp
from jax import lax
from jax.experimental import pallas as pl
from jax.experimental.pallas import tpu as pltpu

EPS = 1e-6
M_HEADS = 4
M_HEAD_DIM = 128
M_WIDTH = M_HEADS * M_HEAD_DIM
CONV_W = 4
CHUNK = 128
A_HEADS = 8
A_HEAD_DIM = 64
A_WIDTH = A_HEADS * A_HEAD_DIM
DIL_PATTERNS = ((128, 1), (512, 4), (2048, 16))
N_BUCKETS = 32
MAX_DISTANCE = 2048
PEER_HEADS = 8
N_KEYS = 128
PEER_QDIM = 256
PEER_TOPK = 16

LANES = 128
SUBLANES = 8
VMEM_LIMIT = 56 * 1024 * 1024

BF16 = jnp.bfloat16
F32 = jnp.float32
HI = lax.Precision.HIGHEST


def _cparams(sem):
    return pltpu.CompilerParams(dimension_semantics=sem, vmem_limit_bytes=VMEM_LIMIT)


def _dot(a, b):
    return jnp.dot(a, b, preferred_element_type=F32)


def _dot_nt(a, b):
    return lax.dot_general(a, b, (((1,), (1,)), ((), ())), preferred_element_type=F32)


def _dot_tn(a, b):
    return lax.dot_general(a, b, (((0,), (0,)), ((), ())), preferred_element_type=F32)


def _inproj_kernel(x_ref, g1_ref, wm_ref, wa_ref, wgt_ref, gb_ref, bd_ref, qkg_ref,
                   uvz_ref, q_ref, k_ref, v_ref, grow_ref):
    x = x_ref[...]
    h = x * lax.rsqrt(jnp.mean(x * x, axis=-1, keepdims=True) + EPS) * g1_ref[...]
    hb = h.astype(BF16)
    uvz_ref[...] = _dot(hb, wm_ref[...])
    qkv = _dot(hb, wa_ref[...])
    bd = bd_ref[...]

    def head_norm(t, g):
        ms = _dot((t * t).astype(BF16), bd) * (1.0 / A_HEAD_DIM)
        return t * lax.rsqrt(ms + EPS) * g

    q_ref[...] = head_norm(qkv[:, :A_WIDTH], qkg_ref[0:1, :])
    k_ref[...] = head_norm(qkv[:, A_WIDTH:2 * A_WIDTH], qkg_ref[1:2, :])
    v_ref[...] = qkv[:, 2 * A_WIDTH:]
    gr = _dot_nt(wgt_ref[...], hb) + gb_ref[...]
    lf = jnp.minimum(gr, 0.0) - jnp.log1p(jnp.exp(-jnp.abs(gr)))
    row = lax.broadcasted_iota(jnp.int32, gr.shape, 0)
    grow_ref[...] = jnp.where(row < M_HEADS, gr, lf)


def _inproj(x2d, g1, wm, wa, wgt, gb, bd, qkg, tm):
    T, D = x2d.shape
    full = lambda a: pl.BlockSpec(a.shape, lambda i: (0,) * a.ndim)
    return pl.pallas_call(
        _inproj_kernel,
        grid=(T // tm,),
        in_specs=[pl.BlockSpec((tm, D), lambda i: (i, 0)), full(g1), full(wm), full(wa), full(wgt),
                  full(gb), full(bd), full(qkg)],
        out_specs=[pl.BlockSpec((tm, 3 * M_WIDTH), lambda i: (i, 0)),
                   pl.BlockSpec((tm, A_WIDTH), lambda i: (i, 0)),
                   pl.BlockSpec((tm, A_WIDTH), lambda i: (i, 0)),
                   pl.BlockSpec((tm, A_WIDTH), lambda i: (i, 0)),
                   pl.BlockSpec((2 * M_HEADS, tm), lambda i: (0, i))],
        out_shape=[jax.ShapeDtypeStruct((T, 3 * M_WIDTH), F32),
                   jax.ShapeDtypeStruct((T, A_WIDTH), F32),
                   jax.ShapeDtypeStruct((T, A_WIDTH), F32),
                   jax.ShapeDtypeStruct((T, A_WIDTH), F32),
                   jax.ShapeDtypeStruct((2 * M_HEADS, T), F32)],
        compiler_params=_cparams(("parallel",)),
        name="inproj",
    )(x2d, g1, wm, wa, wgt, gb, bd, qkg)


def _mlstm_kernel(*refs):
    nb = (len(refs) - 8) // (2 + 2 * M_HEADS)
    uvz_all, grow_refs = refs[0], refs[1:1 + nb]
    cw_ref, cb_ref, wq_ref, wk_ref, ng_ref, sk_ref, ym_all = refs[1 + nb:8 + nb]
    ubuf_refs = refs[8 + nb:8 + 2 * nb]
    cst_refs = refs[8 + 2 * nb:8 + 2 * nb + nb * M_HEADS]
    mst_refs = refs[8 + 2 * nb + nb * M_HEADS:]
    L = CHUNK

    @pl.when(pl.program_id(0) == 0)
    def _():
        for r in ubuf_refs:
            r[0:SUBLANES, :] = jnp.zeros((SUBLANES, M_WIDTH), F32)
        for r in cst_refs + mst_refs:
            r[...] = jnp.zeros_like(r)

    rr = lax.broadcasted_iota(jnp.int32, (L, L), 0)
    cc = lax.broadcasted_iota(jnp.int32, (L, L), 1)
    causal = cc <= rr
    lane8 = lax.broadcasted_iota(jnp.int32, (2 * M_HEADS, L), 1)
    seqs = range(nb)
    chains = [(b, h) for b in seqs for h in range(M_HEADS)]
    cols = lambda h, off=0: slice(off + h * M_HEAD_DIM, off + (h + 1) * M_HEAD_DIM)

    cact, gr, b_rows = {}, {}, {}
    for b in seqs:
        uvz_ref, ubuf_ref = uvz_all.at[b], ubuf_refs[b]
        u = uvz_ref[:, 0:M_WIDTH]
        ubuf_ref[SUBLANES:SUBLANES + L, :] = u
        conv = cb_ref[...] + jnp.zeros((L, M_WIDTH), F32)
        for w in range(CONV_W):
            conv = conv + ubuf_ref[pl.ds(SUBLANES - (CONV_W - 1) + w, L), :] * cw_ref[w:w + 1, :]
        ubuf_ref[0:SUBLANES, :] = u[L - SUBLANES:, :]
        cact[b] = conv * jax.nn.sigmoid(conv)
        gr[b] = grow_refs[b][...]
    ch, qb, kb, kf, vaug = {}, {}, {}, {}, {}
    for c in chains:
        b, h = c
        ch[c] = cact[b][:, cols(h)]
        chb = ch[c].astype(BF16)
        qb[c] = _dot(chb, wq_ref[h]).astype(BF16)
        kf[c] = _dot(chb, wk_ref[h]) * (M_HEAD_DIM ** -0.5)
        kb[c] = kf[c].astype(BF16)
        v = uvz_all[b, :, cols(h, M_WIDTH)]
        vaug[c] = jnp.concatenate([v, jnp.ones((L, M_HEAD_DIM), F32)], axis=1).astype(BF16)

    for b in seqs:
        b_rows[b] = gr[b]
    shift = 1
    while shift < L:
        for b in seqs:
            b_rows[b] = b_rows[b] + jnp.where(lane8 >= shift, pltpu.roll(b_rows[b], shift, 1), 0.0)
        shift *= 2
    b_up = {b: pltpu.roll(b_rows[b], M_HEADS, 0) for b in seqs}
    rmax = {b: gr[b] - b_up[b] for b in seqs}
    shift = 1
    while shift < L:
        for b in seqs:
            rmax[b] = jnp.maximum(rmax[b], jnp.where(lane8 >= shift, pltpu.roll(rmax[b], shift, 1), -jnp.inf))
        shift *= 2
    pieces = {}
    for b in seqs:
        xs = jnp.concatenate([b_rows[b], gr[b], rmax[b] + b_up[b]], axis=0)
        hi = xs.astype(BF16)
        r1 = xs - hi.astype(F32)
        mid = r1.astype(BF16)
        pieces[b] = (hi, mid, (r1 - mid.astype(F32)).astype(BF16))
    sel_r = lax.broadcasted_iota(jnp.int32, (3 * 2 * M_HEADS, 3 * LANES), 0)
    sel_c = lax.broadcasted_iota(jnp.int32, (3 * 2 * M_HEADS, 3 * LANES), 1)

    b_c, li_c, b_last, m_prev, caug, m_t, d_mat, inter_w = {}, {}, {}, {}, {}, {}, {}, {}
    for c in chains:
        b, h = c
        want = jnp.where(sel_c < LANES, M_HEADS + h, jnp.where(sel_c < 2 * LANES, 2 * M_HEADS + h, 4 * M_HEADS + h))
        sel = (sel_r == want).astype(BF16)
        rep = _dot_tn(pieces[b][0], sel) + _dot_tn(pieces[b][1], sel) + _dot_tn(pieces[b][2], sel)
        b_c[c], li_c[c], row_max = rep[:, :LANES], rep[:, LANES:2 * LANES], rep[:, 2 * LANES:]
        b_r = b_rows[b][M_HEADS + h:M_HEADS + h + 1, :]
        li_r = gr[b][h:h + 1, :]
        b_last[c] = b_c[c][L - 1:L, :]
        m_prev[c] = mst_refs[b * M_HEADS + h][0:1, :]
        caug[c] = cst_refs[b * M_HEADS + h][...]
        log_d = b_c[c] - b_r + li_r
        inter = b_c[c] + m_prev[c]
        m_t[c] = jnp.maximum(inter, row_max)
        d_mat[c] = jnp.where(causal, jnp.exp(log_d - m_t[c]), 0.0)
        inter_w[c] = jnp.exp(inter - m_t[c])

    twice = lambda x: jnp.concatenate([x, x], axis=1)
    s = {c: _dot_nt(qb[c], kb[c]) * d_mat[c] for c in chains}
    qc = {c: _dot(qb[c], caug[c].astype(BF16)) for c in chains}
    nd = {c: _dot(s[c].astype(BF16), vaug[c]) + twice(inter_w[c]) * qc[c] for c in chains}

    for c in chains:
        b, h = c
        g_c = b_last[c] - b_c[c] + li_c[c]
        m_loc = jnp.max(g_c, axis=0, keepdims=True)
        kw = (jnp.exp(g_c - m_loc) * kf[c]).astype(BF16)
        c_new = _dot_tn(kw, vaug[c])
        m_new = jnp.maximum(b_last[c] + m_prev[c], m_loc)
        a = jnp.exp(b_last[c] + m_prev[c] - m_new)
        bb = jnp.exp(m_loc - m_new)
        cst_refs[b * M_HEADS + h][...] = twice(a) * caug[c] + twice(bb) * c_new
        mst_refs[b * M_HEADS + h][...] = jnp.broadcast_to(m_new, (SUBLANES, LANES))

    mean_mat = jnp.full((M_HEAD_DIM, M_HEAD_DIM), 1.0 / M_HEAD_DIM, BF16)
    for c in chains:
        b, h = c
        num, den = nd[c][:, :M_HEAD_DIM], nd[c][:, M_HEAD_DIM:]
        hval = num / jnp.maximum(jnp.abs(den), jnp.exp(-m_t[c]))
        hn = hval * lax.rsqrt(_dot((hval * hval).astype(BF16), mean_mat) + EPS) * ng_ref[:, cols(h)]
        hn = hn + sk_ref[:, cols(h)] * ch[c]
        z = uvz_all[b, :, cols(h, 2 * M_WIDTH)]
        ym_all[b, :, cols(h)] = jax.nn.sigmoid(z) * hn


def _mlstm(uvz, grow, cw, cb, wq, wk, ng, sk, B, S):
    nc = S // CHUNK
    full = lambda a: pl.BlockSpec(a.shape, lambda c: (0,) * a.ndim)
    gate_specs = [pl.BlockSpec((2 * M_HEADS, CHUNK), lambda c, b=b: (0, b * nc + c)) for b in range(B)]
    ym = pl.pallas_call(
        _mlstm_kernel,
        grid=(nc,),
        in_specs=[pl.BlockSpec((B, CHUNK, 3 * M_WIDTH), lambda c: (0, c, 0))] + gate_specs
                 + [full(cw), full(cb), full(wq), full(wk), full(ng), full(sk)],
        out_specs=pl.BlockSpec((B, CHUNK, M_WIDTH), lambda c: (0, c, 0)),
        out_shape=jax.ShapeDtypeStruct((B, S, M_WIDTH), F32),
        scratch_shapes=([pltpu.VMEM((SUBLANES + CHUNK, M_WIDTH), F32)] * B
                        + [pltpu.VMEM((M_HEAD_DIM, 2 * M_HEAD_DIM), F32)] * (B * M_HEADS)
                        + [pltpu.VMEM((SUBLANES, LANES), F32)] * (B * M_HEADS)),
        compiler_params=_cparams(("arbitrary",)),
        name="mlstm",
    )(uvz.reshape(B, S, 3 * M_WIDTH), *([grow] * B), cw, cb, wq, wk, ng, sk)
    return ym.reshape(B * S, M_WIDTH)


ATT_BLK = DIL_PATTERNS[0][0] // DIL_PATTERNS[0][1]
ATT_SPAN = DIL_PATTERNS[-1][0]


ATT_R = DIL_PATTERNS[1][1]
assert [d for _, d in DIL_PATTERNS] == [1, ATT_R, ATT_R * ATT_R]
ATT_UNROLL = 16


def _dilattn_kernel(q_ref, kp_ref, kc_ref, vp_ref, vc_ref, bias_ref, o_ref,
                    q4_ref, k4_ref, v4_ref, k1_ref, v1_ref, a1_ref, a4_ref):
    blk, span, R = ATT_BLK, ATT_SPAN, ATT_R
    sub = span // R
    n = pl.program_id(1)
    scale = A_HEAD_DIM ** -0.5
    for r in range(R):
        res = pl.ds(r, sub, stride=R)
        q4_ref[r] = q_ref[res, :] * scale
        k4_ref[r, 0:sub] = kp_ref[res, :]
        k4_ref[r, sub:] = kc_ref[res, :]
        v4_ref[r, 0:sub] = vp_ref[res, :]
        v4_ref[r, sub:] = vc_ref[res, :]
    k1_ref[0:blk] = kp_ref[span - blk:, :]
    k1_ref[blk:] = kc_ref[...]
    v1_ref[0:blk] = vp_ref[span - blk:, :]
    v1_ref[blk:] = vc_ref[...]
    first_q =lax.broadcasted_iota(jnp.int32, (blk, LANES), 1) < A_HEAD_DIM

    def attend(q, kk, vv, bias_sel):
        kb = kk.astype(BF16)
        vb = jnp.concatenate([vv, jnp.ones(vv.shape, F32)], axis=1).astype(BF16)
        ms, ls, os_ = [], [], []
        for s_ in range(2):
            qh = jnp.where(first_q, q, 0.0) if s_ == 0 else jnp.where(first_q, 0.0, q)
            s = _dot_nt(qh.astype(BF16), kb) + bias_sel(s_)
            m = jnp.broadcast_to(jnp.max(s, axis=1, keepdims=True), (blk, LANES))
            e = jnp.exp(s - jnp.concatenate([m, m], axis=1))
            ol = _dot(e.astype(BF16), vb)
            ms.append(m)
            os_.append(ol[:, :LANES])
            ls.append(ol[:, LANES:])
        return (jnp.where(first_q, ms[0], ms[1]), jnp.where(first_q, ls[0], ls[1]),
                jnp.where(first_q, os_[0], os_[1]))

    def unit_d1(m, carry):
        base = pl.multiple_of(m * blk, blk)
        first = jnp.logical_and(n == 0, m == 0).astype(jnp.int32)
        m_c, l_c, o_c = attend(q_ref[pl.ds(base, blk), :] * scale, k1_ref[pl.ds(base, 2 * blk), :],
                               v1_ref[pl.ds(base, 2 * blk), :], lambda s_: bias_ref[0, first, s_])
        a1_ref[0, pl.ds(base, blk), :] = m_c
        a1_ref[1, pl.ds(base, blk), :] = l_c
        a1_ref[2, pl.ds(base, blk), :] = o_c
        return carry

    def unit_d4(u, carry):
        m4, r = u >> (R.bit_length() - 1), u & (R - 1)
        off = pl.multiple_of(m4 * blk, blk)
        first = jnp.logical_and(n == 0, m4 == 0).astype(jnp.int32)
        keys = pl.ds(sub - blk + off, 2 * blk)
        m_c, l_c, o_c = attend(q4_ref[r, pl.ds(off, blk), :], k4_ref[r, keys, :], v4_ref[r, keys, :],
                               lambda s_: bias_ref[1, first, s_])
        a4_ref[0, r, pl.ds(off, blk), :] = m_c
        a4_ref[1, r, pl.ds(off, blk), :] = l_c
        a4_ref[2, r, pl.ds(off, blk), :] = o_c
        return carry

    def unit_d16(u, carry):
        r, rp = u & (R - 1), u >> (R.bit_length() - 1)
        first = (n == 0).astype(jnp.int32)
        own, cur = pl.ds(rp, blk, stride=R), pl.ds(sub + rp, blk, stride=R)
        kk = jnp.concatenate([k4_ref[r, own, :], k4_ref[r, cur, :]], axis=0)
        vv = jnp.concatenate([v4_ref[r, own, :], v4_ref[r, cur, :]], axis=0)
        m_c, l_c, o_c = attend(q4_ref[r, own, :], kk, vv, lambda s_: bias_ref[2, first, s_])
        m_o = a4_ref[0, r, own, :]
        m_n = jnp.maximum(m_o, m_c)
        a_o, a_c = jnp.exp(m_o - m_n), jnp.exp(m_c - m_n)
        a4_ref[0, r, own, :] = m_n
        a4_ref[1, r, own, :] = a_o * a4_ref[1, r, own, :] + a_c * l_c
        a4_ref[2, r, own, :] = a_o * a4_ref[2, r, own, :] + a_c * o_c
        return carry

    units = span // blk
    lax.fori_loop(0, units, unit_d1, 0, unroll=ATT_UNROLL)
    lax.fori_loop(0, units, unit_d4, 0, unroll=ATT_UNROLL)
    lax.fori_loop(0, units, unit_d16, 0, unroll=ATT_UNROLL)

    for r in range(R):
        res = pl.ds(r, sub, stride=R)
        m1, m4 = a1_ref[0, res, :], a4_ref[0, r]
        m_n = jnp.maximum(m1, m4)
        e1, e4 = jnp.exp(m1 - m_n), jnp.exp(m4 - m_n)
        o_ref[res, :] = ((e1 * a1_ref[2, res, :] + e4 * a4_ref[2, r])
                         / (e1 * a1_ref[1, res, :] + e4 * a4_ref[1, r]))


def _dilattn(q, k, v, bias, B, S):
    span, blk, R = ATT_SPAN, ATT_BLK, ATT_R
    ns = S // span
    T = B * S
    cur = pl.BlockSpec((span, LANES), lambda b, n, hp: (b * ns + n, hp))
    prev = pl.BlockSpec((span, LANES), lambda b, n, hp: (b * ns + jnp.maximum(n - 1, 0), hp))
    return pl.pallas_call(
        _dilattn_kernel,
        grid=(B, ns, A_HEADS // 2),
        in_specs=[cur, prev, cur, prev, cur,
                  pl.BlockSpec((len(DIL_PATTERNS), 2, 2, blk, 2 * blk), lambda b, n, hp: (0, 0, hp, 0, 0))],
        out_specs=cur,
        out_shape=jax.ShapeDtypeStruct((T, A_WIDTH), F32),
        scratch_shapes=[pltpu.VMEM((R, span // R, LANES), F32),
                        pltpu.VMEM((R, 2 * span // R, LANES), F32),
                        pltpu.VMEM((R, 2 * span // R, LANES), F32),
                        pltpu.VMEM((blk + span, LANES), F32),
                        pltpu.VMEM((blk + span, LANES), F32),
                        pltpu.VMEM((3, span, LANES), F32),
                        pltpu.VMEM((3, R, span // R, LANES), F32)],
        compiler_params=_cparams(("parallel", "parallel", "parallel")),
        name="dilattn",
    )(q, k, k, v, v, bias)


def _attn_bias_kernel(bkt_ref, rb_ref, out_ref):
    h = pl.program_id(1)
    bkt = bkt_ref[...]
    acc = jnp.full(bkt.shape, -jnp.inf, F32)
    for kk in range(N_BUCKETS):
        acc = jnp.where(bkt == kk, rb_ref[kk, h], acc)
    col = lax.broadcasted_iota(jnp.int32, bkt.shape, 1)
    out_ref[0] = acc
    out_ref[1] = jnp.where(col < ATT_BLK, -jnp.inf, acc)


def _attn_bias(rel_bias):
    blk = ATT_BLK
    i = jnp.arange(blk)[:, None]
    j = jnp.arange(2 * blk)[None, :]
    steps = i + blk - j
    band = (steps >= 0) & (steps <= blk)
    bkt = jnp.stack([jnp.where(band, _t5_bucket(jnp.maximum(steps, 0) * d), -1) for _, d in DIL_PATTERNS])
    P = len(DIL_PATTERNS)
    return pl.pallas_call(
        _attn_bias_kernel,
        grid=(P, A_HEADS),
        in_specs=[pl.BlockSpec((None, blk, 2 * blk), lambda p, h: (p, 0, 0)),
                  pl.BlockSpec(memory_space=pltpu.SMEM)],
        out_specs=pl.BlockSpec((None, 2, None, blk, 2 * blk), lambda p, h: (p, 0, h, 0, 0)),
        out_shape=jax.ShapeDtypeStruct((P, 2, A_HEADS, blk, 2 * blk), F32),
        compiler_params=_cparams(("parallel", "parallel")),
        name="attn_bias",
    )(bkt.astype(jnp.int32), rel_bias)


def _t5_bucket(dist):
    max_exact = N_BUCKETS // 2
    nf = jnp.maximum(dist, 1).astype(F32)
    large = max_exact + (jnp.log(nf / max_exact) / np.log(MAX_DISTANCE / max_exact)
                         * (N_BUCKETS - max_exact)).astype(jnp.int32)
    large = jnp.minimum(large, N_BUCKETS - 1)
    return jnp.where(dist < max_exact, dist, large)


def _outproj_kernel(x_ref, ym_ref, ya_ref, wo_ref, g2_ref, x2_ref, h2t_ref):
    y = (_dot(ym_ref[...].astype(BF16), wo_ref[0:M_WIDTH, :])
         + _dot(ya_ref[...].astype(BF16), wo_ref[M_WIDTH:, :]))
    x2 = x_ref[...] + y
    x2_ref[...] = x2
    h2 = x2 * lax.rsqrt(jnp.mean(x2 * x2, axis=-1, keepdims=True) + EPS) * g2_ref[...]
    h2t_ref[...] = jnp.transpose(h2).astype(BF16)


def _outproj(x2d, ym, ya, wo, g2, tm):
    T, D = x2d.shape
    half = pl.BlockSpec((tm, M_WIDTH), lambda i: (i, 0))
    full = lambda a: pl.BlockSpec(a.shape, lambda i: (0,) * a.ndim)
    return pl.pallas_call(
        _outproj_kernel,
        grid=(T // tm,),
        in_specs=[pl.BlockSpec((tm, D), lambda i: (i, 0)), half, half, full(wo), full(g2)],
        out_specs=[pl.BlockSpec((tm, D), lambda i: (i, 0)), pl.BlockSpec((D, tm), lambda i: (0, i))],
        out_shape=[jax.ShapeDtypeStruct((T, D), F32), jax.ShapeDtypeStruct((D, T), BF16)],
        compiler_params=_cparams(("parallel",)),
        name="outproj",
    )(x2d, ym, ya, wo, g2)


def _peer_keys_kernel(keys_ref, wqt_ref, out_ref):
    out_ref[...] = jnp.dot(keys_ref[...], wqt_ref[...], preferred_element_type=F32,
                           precision=HI).astype(out_ref.dtype)


def _peer_keys(keys, wqt):
    _, H, K, C = keys.shape
    D = wqt.shape[-1]
    return pl.pallas_call(
        _peer_keys_kernel,
        grid=(2, H),
        in_specs=[pl.BlockSpec((None, None, K, C), lambda a, h: (a, h, 0, 0)),
                  pl.BlockSpec((None, None, C, D), lambda a, h: (a, h, 0, 0))],
        out_specs=pl.BlockSpec((None, None, K, D), lambda a, h: (a, h, 0, 0)),
        out_shape=jax.ShapeDtypeStruct((2, H, K, D), BF16),
        compiler_params=_cparams(("parallel", "parallel")),
        name="peer_keys",
    )(keys, wqt)


def _oddeven_merge_sort_pairs(n):
    pairs = []

    def merge(lo, hi, r):
        step = r * 2
        if step < hi - lo:
            merge(lo, hi, step)
            merge(lo + r, hi, step)
            for i in range(lo + r, hi - r, step):
                pairs.append((i, i + r))
        else:
            pairs.append((lo, lo + r))

    def sort(lo, hi):
        if hi - lo >= 1:
            mid = lo + (hi - lo) // 2
            sort(lo, mid)
            sort(mid + 1, hi)
            merge(lo, hi, 1)

    sort(0, n - 1)
    return pairs


_SORT16 = _oddeven_merge_sort_pairs(PEER_TOPK)


def _sort_desc(w):
    w = list(w)
    for a, b in _SORT16:
        hi, lo = jnp.maximum(w[a], w[b]), jnp.minimum(w[a], w[b])
        w[a], w[b] = hi, lo
    return w


def _merge_top(a, b):
    n = len(a)
    w = [jnp.maximum(a[i], b[n - 1 - i]) for i in range(n)]
    half = n // 2
    while half >= 1:
        for start in range(0, n, 2 * half):
            for i in range(start, start + half):
                hi, lo = jnp.maximum(w[i], w[i + half]), jnp.minimum(w[i], w[i + half])
                w[i], w[i + half] = hi, lo
        half //= 2
    return w


def _peer_topk_kernel(wk_ref, h2t_ref, ub_ref, nb_ref, vb_ref, rk_ref, sc_ref):
    H, K = PEER_HEADS, N_KEYS
    tb = h2t_ref.shape[1]
    h2t = h2t_ref[...]
    for a in range(2):
        for h in range(H):
            sc_ref[a, h] = _dot(wk_ref[a, h], h2t)

    sub = lax.broadcasted_iota(jnp.int32, (SUBLANES, LANES), 0)
    zeros = jnp.zeros((SUBLANES, LANES), F32)
    for lt in range(tb // LANES):
        ls = slice(lt * LANES, (lt + 1) * LANES)
        packed = []
        for a in range(2):
            acc = [None] * PEER_TOPK
            for h in range(H):
                w = [sc_ref[a, h, v * SUBLANES:(v + 1) * SUBLANES, ls] for v in range(K // SUBLANES)]
                w = _sort_desc(w)
                for shift in (4, 2, 1):
                    w = _merge_top(w, [pltpu.roll(x, shift, 0) for x in w])
                for i in range(PEER_TOPK):
                    acc[i] = w[i] if h == 0 else jnp.where(sub == h, w[i], acc[i])
            packed.append(acc)
        v1, v2 = packed
        rows = [[v1[a] + v2[b] for b in range(PEER_TOPK // (a + 1))] for a in range(PEER_TOPK)]
        neg = jnp.full((SUBLANES, LANES), -jnp.inf, F32)
        l0 = rows[0]
        l1 = _sort_desc(rows[1] + rows[2] + rows[4])
        l2 = _sort_desc(rows[3] + rows[5] + rows[6] + rows[7] + [rows[a][0] for a in range(8, 14)])
        l3 = [rows[14][0], rows[15][0]]
        l3 = [jnp.maximum(l3[0], l3[1]), jnp.minimum(l3[0], l3[1])] + [neg] * (PEER_TOPK - 2)
        vc = _merge_top(_merge_top(l0, l1), _merge_top(l2, l3))
        top, tau = vc[0], vc[PEER_TOPK - 1]
        zsum = zeros
        for i in range(PEER_TOPK):
            zsum = zsum + jnp.exp(vc[i] - top)
        inv_z = 1.0 / zsum
        n_sel = []
        for a in range(PEER_TOPK):
            cnt = zeros
            for cand in rows[a]:
                cnt = cnt + jnp.where(cand >= tau, 1.0, 0.0)
            n_sel.append(cnt)
        m1, m2 = v1[0], v2[0]
        for h in range(H):
            bc = lambda x: jnp.broadcast_to(x[h:h + 1, :], (SUBLANES, LANES))
            w1 = [bc(x) for x in v1]
            w2 = [bc(x) for x in v2]
            na = [bc(x) for x in n_sel]
            m1h, m2h, izh = bc(m1), bc(m2), bc(inv_z)
            ub, nb, vb, rk = [], [], [], []
            for v in range(K // SUBLANES):
                s1 = sc_ref[0, h, v * SUBLANES:(v + 1) * SUBLANES, ls]
                s2 = sc_ref[1, h, v * SUBLANES:(v + 1) * SUBLANES, ls]
                nbv = zeros
                for a in range(PEER_TOPK):
                    nbv = jnp.where(s1 == w1[a], na[a], nbv)
                rkv, step, taken = zeros, PEER_TOPK // 2, []
                while step >= 1:
                    level = w2[step - 1:PEER_TOPK - 1:2 * step]
                    for bit in reversed(taken):
                        level = [jnp.where(bit, hi_, lo_) for lo_, hi_ in zip(level[0::2], level[1::2])]
                    above = level[0] > s2
                    taken.append(above)
                    rkv = rkv + jnp.where(above, float(step), 0.0)
                    step //= 2
                rkv = jnp.where(w2[PEER_TOPK - 1] > s2, float(PEER_TOPK), rkv)
                ub.append(jnp.exp(s1 - m1h) * izh * 0.5)
                vb.append(jnp.exp(s2 - m2h))
                nb.append(nbv)
                rk.append(rkv)
            ub_ref[h, :, ls] = jnp.concatenate(ub, axis=0)
            nb_ref[h, :, ls] = jnp.concatenate(nb, axis=0)
            vb_ref[h, :, ls] = jnp.concatenate(vb, axis=0).astype(BF16)
            rk_ref[h, :, ls] = jnp.concatenate(rk, axis=0).astype(BF16)


def _peer_topk(wk, h2t, tb):
    D, T = h2t.shape
    H, K = PEER_HEADS, N_KEYS
    spec = pl.BlockSpec((H, K, tb), lambda i: (0, 0, i))
    return pl.pallas_call(
        _peer_topk_kernel,
        grid=(T // tb,),
        in_specs=[pl.BlockSpec(wk.shape, lambda i: (0, 0, 0, 0)), pl.BlockSpec((D, tb), lambda i: (0, i))],
        out_specs=[spec] * 4,
        out_shape=[jax.ShapeDtypeStruct((H, K, T), F32)] * 2 + [jax.ShapeDtypeStruct((H, K, T), BF16)] * 2,
        scratch_shapes=[pltpu.VMEM((2, H, K, tb), F32)],
        compiler_params=_cparams(("parallel",)),
        name="peer_topk",
    )(wk, h2t)


def _peer_main_kernel(h2t_ref, ue_ref, vet_ref, ub_ref, nb_ref, vb_ref, rk_ref, x2_ref, out_ref,
                      acc_ref, a_ref):
    c = pl.program_id(1)
    n_i = ue_ref.shape[0] // N_KEYS

    @pl.when(c == 0)
    def _():
        acc_ref[...] = jnp.zeros_like(acc_ref)

    h2t = h2t_ref[...]
    for ii in range(n_i):
        rs = slice(ii * N_KEYS, (ii + 1) * N_KEYS)
        pre = _dot(ue_ref[rs, :], h2t)
        act = (pre * (lax.erf(pre * (2.0 ** -0.5)) + 1.0)).astype(BF16)
        half = pre.shape[1] // 2
        for ts in (slice(0, half), slice(half, 2 * half)):
            gate = jnp.zeros((N_KEYS, half), BF16)
            for h in range(PEER_HEADS):
                row = lambda ref: jnp.broadcast_to(ref[h, ii:ii + 1, ts], (N_KEYS, half)).astype(BF16)
                gate = gate + jnp.where(rk_ref[h, :, ts] < row(nb_ref), row(ub_ref) * vb_ref[h, :, ts], 0)
            a_ref[rs, ts] = act[:, ts] * gate
    acc_ref[...] += _dot(vet_ref[...], a_ref[...])

    @pl.when(c == pl.num_programs(1) - 1)
    def _():
        out_ref[...] = x2_ref[...] + jnp.transpose(acc_ref[...])


def _peer_main(h2t, ue, vet, ub, nb, vb, rk, x2, tb, ec):
    D, T = h2t.shape
    E = ue.shape[0]
    H, K = PEER_HEADS, N_KEYS
    row = pl.BlockSpec((H, ec // K, tb), lambda t, c: (0, c, t))
    col = pl.BlockSpec((H, K, tb), lambda t, c: (0, 0, t))
    return pl.pallas_call(
        _peer_main_kernel,
        grid=(T // tb, E // ec),
        in_specs=[pl.BlockSpec((D, tb), lambda t, c: (0, t)),
                  pl.BlockSpec((ec, D), lambda t, c: (c, 0)),
                  pl.BlockSpec((D, ec), lambda t, c: (0, c)),
                  row, row, col, col,
                  pl.BlockSpec((tb, D), lambda t, c: (t, 0))],
        out_specs=pl.BlockSpec((tb, D), lambda t, c: (t, 0)),
        out_shape=jax.ShapeDtypeStruct((T, D), F32),
        scratch_shapes=[pltpu.VMEM((D, tb), F32), pltpu.VMEM((ec, tb), BF16)],
        compiler_params=_cparams(("parallel", "arbitrary")),
        name="peer_main",
    )(h2t, ue, vet, ub, nb, vb, rk, x2)


def _block_rows(T, want):
    tm = min(T, want)
    assert T % tm == 0
    return tm


PROJ_ROWS = 512
TOPK_TOKENS = 256
PEER_TOKENS = 512
PEER_EXPERTS = 2048


def _layer(x2d, B, S, norm1_g, w_in, conv_w, conv_b, wq_m, wk_m, ig_b, fg_b, mh_norm_g, skip_m,
           qn_g, kn_g, rel_bias, w_out, norm2_g, w_query, sub_keys1, sub_keys2, expert_u, expert_v):
    T, D = x2d.shape
    o_vm, o_z, o_i, o_f, o_q = M_WIDTH, 2 * M_WIDTH, 3 * M_WIDTH, 3 * M_WIDTH + M_HEADS, 3 * M_WIDTH + 2 * M_HEADS
    wm = w_in[:, :o_i].astype(BF16)
    wa = w_in[:, o_q:].astype(BF16)
    wgt = jnp.transpose(w_in[:, o_i:o_q]).astype(BF16)
    gb = jnp.concatenate([ig_b, fg_b])[:, None]
    seg = np.arange(A_WIDTH) // A_HEAD_DIM
    bd = jnp.asarray(seg[:, None] == seg[None, :], BF16)
    qkg = jnp.stack([qn_g.reshape(-1), kn_g.reshape(-1)])
    tm = _block_rows(T, PROJ_ROWS)
    uvz, qn, kn, va, grow = _inproj(x2d, norm1_g[None, :], wm, wa, wgt, gb, bd, qkg, tm)

    ym = _mlstm(uvz, grow, conv_w, conv_b[None, :], wq_m.astype(BF16), wk_m.astype(BF16),
                mh_norm_g.reshape(1, -1), skip_m.reshape(1, -1), B, S)

    ya = _dilattn(qn, kn, va, _attn_bias(rel_bias), B, S)

    x2, h2t = _outproj(x2d, ym, ya, w_out.astype(BF16), norm2_g[None, :], tm)

    keys = jnp.stack([sub_keys1, sub_keys2])
    wqt = jnp.transpose(w_query.reshape(D, PEER_HEADS, 2, PEER_QDIM // 2), (2, 1, 3, 0))
    wk = _peer_keys(keys, wqt)
    ub, nb, vb, rk = _peer_topk(wk, h2t, _block_rows(T, TOPK_TOKENS))
    ue = expert_u.astype(BF16)
    vet = jnp.transpose(expert_v).astype(BF16)
    return _peer_main(h2t, ue, vet, ub, nb, vb, rk, x2, _block_rows(T, PEER_TOKENS), PEER_EXPERTS)


def kernel(x, norm1_g, w_in, conv_w, conv_b, wq_m, wk_m, ig_b, fg_b, mh_norm_g, skip_m, qn_g, kn_g,
           rel_bias, w_out, norm2_g, w_query, sub_keys1, sub_keys2, expert_u, expert_v):
    B, S, D = x.shape
    assert S % DIL_PATTERNS[-1][0] == 0 and S % CHUNK == 0
    x2d = x.reshape(B * S, D)
    for l in range(norm1_g.shape[0]):
        x2d = _layer(x2d, B, S, norm1_g[l], w_in[l], conv_w[l], conv_b[l], wq_m[l], wk_m[l], ig_b[l],
                     fg_b[l], mh_norm_g[l], skip_m[l], qn_g[l], kn_g[l], rel_bias, w_out[l], norm2_g[l],
                     w_query[l], sub_keys1[l], sub_keys2[l], expert_u[l], expert_v[l])
    return x2d.reshape(B, S, D)
```
